```python
import math
import jax
import jax.numpy as jnp
from jax import lax
import numpy as np

D_MODEL = 1024
BATCH = 8
SEQ = 8192
DEPTH = 4

N_HEADS = 8
N_KV_HEADS = 2
HEAD_DIM = D_MODEL // N_HEADS
GROUP = N_HEADS // N_KV_HEADS
Q_DIM = N_HEADS * HEAD_DIM
KV_DIM = N_KV_HEADS * HEAD_DIM
WINDOW = 128
BLOCK = 128
NUM_BUCKETS = 32
MAX_DISTANCE = 128
LRU_WIDTH = D_MODEL
LRU_BLOCKS = 8
LRU_BLOCK_DIM = LRU_WIDTH // LRU_BLOCKS
LRU_C = 8.0
CONV_WIDTH = 4
CONV_LEFT = 2
N_DIRS = 2
D_FF = 4 * D_MODEL
EPS = 1e-6
SPLITS = (Q_DIM, KV_DIM, KV_DIM, LRU_WIDTH, LRU_WIDTH, D_MODEL, D_MODEL)
IN_COLS = sum(SPLITS)

kernel_name = "hybrid_swa_rglru_encoder"


def rms_norm(x, g):
    xf = x.astype(jnp.float32)
    y = xf * lax.rsqrt(jnp.mean(xf * xf, axis=-1, keepdims=True) + EPS)
    return (y * g.astype(jnp.float32)).astype(x.dtype)


def t5_bucket(rel):
    half = NUM_BUCKETS // 2
    max_exact = half // 2
    n = jnp.abs(rel)
    large = max_exact + (jnp.log(jnp.maximum(n, 1).astype(jnp.float32) / max_exact)
                         / math.log(MAX_DISTANCE / max_exact) * (half - max_exact)).astype(jnp.int32)
    large = jnp.minimum(large, half - 1)
    return jnp.where(rel > 0, half, 0) + jnp.where(n < max_exact, n, large)


def band_blocks(t):
    b, s = t.shape[0], t.shape[1]
    nb = s // BLOCK
    tp = jnp.pad(t, ((0, 0), (BLOCK, BLOCK), (0, 0), (0, 0)))
    tb = tp.reshape(b, nb + 2, BLOCK, t.shape[2], t.shape[3])
    return jnp.concatenate([tb[:, :-2], tb[:, 1:-1], tb[:, 2:]], axis=2)


def windowed_gqa(q, k, v, sink, rel_bias):
    b, s = q.shape[0], q.shape[1]
    nb = s // BLOCK
    qb = q.reshape(b, nb, BLOCK, N_KV_HEADS, GROUP, HEAD_DIM)
    kb = band_blocks(k.reshape(b, s, N_KV_HEADS, HEAD_DIM))
    vb = band_blocks(v.reshape(b, s, N_KV_HEADS, HEAD_DIM))
    scores = jnp.einsum('bnqkgd,bnjkd->bnkgqj', qb, kb).astype(jnp.float32) * (HEAD_DIM ** -0.5)
    q_off = jnp.arange(BLOCK, dtype=jnp.int32)[:, None]
    j_off = jnp.arange(3 * BLOCK, dtype=jnp.int32)[None, :] - BLOCK
    rel = j_off - q_off
    bias = rel_bias.astype(jnp.float32)[t5_bucket(rel)]
    bias = jnp.transpose(bias, (2, 0, 1)).reshape(N_KV_HEADS, GROUP, BLOCK, 3 * BLOCK)
    k_abs = jnp.arange(nb, dtype=jnp.int32)[:, None] * BLOCK + j_off
    valid = (jnp.abs(rel) <= WINDOW)[None] & ((k_abs >= 0) & (k_abs < s))[:, None, :]
    logits = jnp.where(valid[None, :, None, None], scores + bias, -jnp.inf)
    sink_l = sink.astype(jnp.float32).reshape(1, 1, N_KV_HEADS, GROUP, 1, 1)
    m = jnp.maximum(jnp.max(logits, axis=-1, keepdims=True), sink_l)
    p = jnp.exp(logits - m)
    denom = jnp.sum(p, axis=-1, keepdims=True) + jnp.exp(sink_l - m)
    p = (p / denom).astype(v.dtype)
    out = jnp.einsum('bnkgqj,bnjkd->bnqkgd', p, vb)
    return out.reshape(b, s, Q_DIM)


def _lin_combine(e1, e2):
    a1, b1 = e1
    a2, b2 = e2
    return a1 * a2, a2 * b1 + b2


def rg_lru_bidir(xr, gate, conv_w, conv_b, w_rg, b_rg, w_ig, b_ig, lam):
    b, s = xr.shape[0], xr.shape[1]
    xf = xr.astype(jnp.float32)
    xp = jnp.pad(xf, ((0, 0), (CONV_LEFT, CONV_WIDTH - 1 - CONV_LEFT), (0, 0)))
    cw = conv_w.astype(jnp.float32)
    xc = conv_b.astype(jnp.float32) + xp[:, 0:s] * cw[0]
    for j in range(1, CONV_WIDTH):
        xc = xc + xp[:, j:j + s] * cw[j]
    xblk = xc.reshape(b, s, LRU_BLOCKS, LRU_BLOCK_DIM)
    outs = []
    for d in range(N_DIRS):
        r = jax.nn.sigmoid(jnp.einsum('bsnh,nhk->bsnk', xblk, w_rg[d].astype(jnp.float32)).reshape(b, s, LRU_WIDTH)
                           + b_rg[d].astype(jnp.float32))
        i = jax.nn.sigmoid(jnp.einsum('bsnh,nhk->bsnk', xblk, w_ig[d].astype(jnp.float32)).reshape(b, s, LRU_WIDTH)
                           + b_ig[d].astype(jnp.float32))
        log_a = -LRU_C * r * jax.nn.softplus(-lam[d].astype(jnp.float32))
        a = jnp.exp(log_a)
        mult = jnp.sqrt(-jnp.expm1(2.0 * log_a))
        _, hs = lax.associative_scan(_lin_combine, (a, mult * (i * xc)), axis=1, reverse=(d == 1))
        outs.append(hs)
    y = outs[0] + outs[1]
    return (y * jax.nn.gelu(gate.astype(jnp.float32))).astype(xr.dtype)


def setup_inputs(seed: int = 0) -> dict:
    key = jax.random.key(seed)
    ks = jax.random.split(key, 20)
    f32 = jnp.float32

    def nrm(k, shape, scale):
        return jax.random.normal(k, shape, f32) * scale

    x = nrm(ks[0], (BATCH, SEQ, D_MODEL), 1.0)
    w_in = nrm(ks[1], (DEPTH, D_MODEL, IN_COLS), D_MODEL ** -0.5)
    w_o_attn = nrm(ks[2], (DEPTH, Q_DIM, D_MODEL), Q_DIM ** -0.5)
    w_o_lru = nrm(ks[3], (DEPTH, LRU_WIDTH, D_MODEL), LRU_WIDTH ** -0.5)
    w_out = nrm(ks[4], (DEPTH, D_MODEL, D_MODEL), D_MODEL ** -0.5)
    attn_sink = nrm(ks[5], (DEPTH, N_HEADS), 0.5)
    rel_bias = nrm(ks[6], (NUM_BUCKETS, N_HEADS), 0.5)
    conv_w = nrm(ks[7], (DEPTH, CONV_WIDTH, LRU_WIDTH), CONV_WIDTH ** -0.5)
    conv_b = nrm(ks[8], (DEPTH, LRU_WIDTH), 0.02)
    w_rgate = nrm(ks[9], (DEPTH, N_DIRS, LRU_BLOCKS, LRU_BLOCK_DIM, LRU_BLOCK_DIM), LRU_BLOCK_DIM ** -0.5)
    b_rgate = nrm(ks[10], (DEPTH, N_DIRS, LRU_WIDTH), 0.02)
    w_igate = nrm(ks[11], (DEPTH, N_DIRS, LRU_BLOCKS, LRU_BLOCK_DIM, LRU_BLOCK_DIM), LRU_BLOCK_DIM ** -0.5)
    b_igate = nrm(ks[12], (DEPTH, N_DIRS, LRU_WIDTH), 0.02)
    a_c = jax.random.uniform(ks[13], (DEPTH, N_DIRS, LRU_WIDTH), f32, 0.9, 0.999)
    a = a_c ** (1.0 / LRU_C)
    lru_lambda = jnp.log(a) - jnp.log1p(-a)
    norm1_g = 1.0 + nrm(ks[14], (DEPTH, D_MODEL), 0.02)
    norm2_g = 1.0 + nrm(ks[15], (DEPTH, D_MODEL), 0.02)
    w_mlp_up = nrm(ks[16], (DEPTH, D_MODEL, D_FF), D_MODEL ** -0.5)
    w_mlp_down = nrm(ks[17], (DEPTH, D_FF, D_MODEL), 0.5 * D_FF ** -0.5)
    final_norm_g = 1.0 + nrm(ks[18], (D_MODEL,), 0.02)
    return {"x": x, "w_in": w_in, "w_o_attn": w_o_attn, "w_o_lru": w_o_lru, "w_out": w_out,
            "attn_sink": attn_sink, "rel_bias": rel_bias, "conv_w": conv_w, "conv_b": conv_b,
            "w_rgate": w_rgate, "b_rgate": b_rgate, "w_igate": w_igate, "b_igate": b_igate,
            "lru_lambda": lru_lambda, "norm1_g": norm1_g, "norm2_g": norm2_g,
            "w_mlp_up": w_mlp_up, "w_mlp_down": w_mlp_down, "final_norm_g": final_norm_g}


def reference(x, w_in, w_o_attn, w_o_lru, w_out, attn_sink, rel_bias, conv_w, conv_b,
              w_rgate, b_rgate, w_igate, b_igate, lru_lambda, norm1_g, norm2_g,
              w_mlp_up, w_mlp_down, final_norm_g):
    cuts = np.cumsum(SPLITS)[:-1].tolist()
    for l in range(DEPTH):
        h = rms_norm(x, norm1_g[l])
        q, k, v, xr, gr, ga, gb = jnp.split(h @ w_in[l], cuts, axis=-1)
        ya = windowed_gqa(q, k, v, attn_sink[l], rel_bias) @ w_o_attn[l]
        yb = rg_lru_bidir(xr, gr, conv_w[l], conv_b[l], w_rgate[l], b_rgate[l],
                          w_igate[l], b_igate[l], lru_lambda[l]) @ w_o_lru[l]
        mix = jax.nn.sigmoid(ga) * ya + jax.nn.sigmoid(gb) * yb
        x = x + mix @ w_out[l]
        h = rms_norm(x, norm2_g[l])
        x = x + jnp.square(jax.nn.relu(h @ w_mlp_up[l])) @ w_mlp_down[l]
    return rms_norm(x, final_norm_g)
```

```python
import functools
import math

import jax
import jax.numpy as jnp
from jax import lax
from jax.experimental import pallas as pl
from jax.experimental.pallas import tpu as pltpu

N_HEADS = 8
N_KV_HEADS = 2
HEAD_DIM = 128
GROUP = N_HEADS // N_KV_HEADS
KV_DIM = N_KV_HEADS * HEAD_DIM
WINDOW = 128
BLOCK = 128
NUM_BUCKETS = 32
MAX_DISTANCE = 128
LRU_BLOCKS = 8
LRU_C = 8.0
CONV_WIDTH = 4
CONV_LEFT = 2
EPS = 1e-6

SUBLANES = 8
BF16_ROWS = 16
VMEM_LIMIT = 56 * 1024 * 1024

F32 = jnp.float32
BF16 = jnp.bfloat16


def _cparams(sem):
    return pltpu.CompilerParams(dimension_semantics=sem, vmem_limit_bytes=VMEM_LIMIT)


def _const_spec(shape):
    nd = len(shape)
    return pl.BlockSpec(shape, lambda *_: (0,) * nd, pipeline_mode=pl.Buffered(1))


def _rms(xv, g):
    ms = jnp.mean(xv * xv, axis=-1, keepdims=True)
    return xv * lax.rsqrt(ms + EPS) * g


def _t5_bucket(rel):
    half = NUM_BUCKETS // 2
    max_exact = half // 2
    n = jnp.abs(rel)
    large = max_exact + (jnp.log(jnp.maximum(n, 1).astype(F32) / max_exact)
                         / math.log(MAX_DISTANCE / max_exact) * (half - max_exact)).astype(jnp.int32)
    large = jnp.minimum(large, half - 1)
    return jnp.where(rel > 0, half, 0) + jnp.where(n < max_exact, n, large)


def _bias_kernel(rb_ref, bucket_ref, o_ref):
    h = pl.program_id(0)
    bk = bucket_ref[...]
    acc = jnp.zeros(bk.shape, F32)
    for b in range(NUM_BUCKETS):
        acc = jnp.where(bk == b, rb_ref[b * N_HEADS + h], acc)
    q = lax.broadcasted_iota(jnp.int32, bk.shape, 0)
    j = lax.broadcasted_iota(jnp.int32, bk.shape, 1)
    rel = j - BLOCK - q
    o_ref[0] = jnp.where(jnp.abs(rel) <= WINDOW, acc, -jnp.inf)


def _bias_table(rel_bias):
    q_off = jnp.arange(BLOCK, dtype=jnp.int32)[:, None]
    j_off = jnp.arange(3 * BLOCK, dtype=jnp.int32)[None, :] - BLOCK
    bucket = _t5_bucket(j_off - q_off)
    return pl.pallas_call(
        _bias_kernel,
        grid=(N_HEADS,),
        in_specs=[pl.BlockSpec(memory_space=pltpu.SMEM),
                  pl.BlockSpec((BLOCK, 3 * BLOCK), lambda h: (0, 0))],
        out_specs=pl.BlockSpec((1, BLOCK, 3 * BLOCK), lambda h: (h, 0, 0)),
        out_shape=jax.ShapeDtypeStruct((N_HEADS, BLOCK, 3 * BLOCK), F32),
        name="bias_table",
    )(rel_bias.astype(F32).reshape(-1), bucket)


def _inproj_kernel(splits, x_ref, g_ref, w_ref, *out_refs):
    h = _rms(x_ref[...], g_ref[...]).astype(BF16)
    c0 = 0
    for o_ref, width in zip(out_refs, splits):
        o_ref[...] = jnp.dot(h, w_ref[:, c0:c0 + width], preferred_element_type=F32).astype(BF16)
        c0 += width


def _inproj(xf, g, w, tm):
    t, d = xf.shape
    splits = (d, 2 * KV_DIM, d, d, d, d)
    assert sum(splits) == w.shape[1]
    return pl.pallas_call(
        functools.partial(_inproj_kernel, splits),
        grid=(t // tm,),
        in_specs=[pl.BlockSpec((tm, d), lambda i: (i, 0)),
                  _const_spec((1, d)),
                  _const_spec(w.shape)],
        out_specs=[pl.BlockSpec((tm, c), lambda i: (i, 0)) for c in splits],
        out_shape=[jax.ShapeDtypeStruct((t, c), BF16) for c in splits],
        compiler_params=_cparams(("parallel",)),
        name="inproj",
    )(xf, g.reshape(1, d), w)


def _attn_kernel(tq, sink_ref, q_ref, kvp_ref, kvc_ref, kvn_ref, bias_ref, o_ref, kv_scr):
    i = pl.program_id(1)
    nt = pl.num_programs(1)
    nblk = tq // BLOCK
    kv_scr[0:BLOCK] = kvp_ref[...]
    kv_scr[BLOCK:BLOCK + tq] = kvc_ref[...]
    kv_scr[BLOCK + tq:] = kvn_ref[...]
    scale = HEAD_DIM ** -0.5
    col = lax.broadcasted_iota(jnp.int32, (GROUP * BLOCK, 3 * BLOCK), 1)
    for j in range(nblk):
        for g in range(N_KV_HEADS):
            heads = [g * GROUP + h for h in range(GROUP)]
            q4 = jnp.concatenate(
                [q_ref[j * BLOCK:(j + 1) * BLOCK, hh * HEAD_DIM:(hh + 1) * HEAD_DIM] for hh in heads], axis=0)
            k = kv_scr[j * BLOCK:(j + 3) * BLOCK, g * HEAD_DIM:(g + 1) * HEAD_DIM]
            v = kv_scr[j * BLOCK:(j + 3) * BLOCK, KV_DIM + g * HEAD_DIM:KV_DIM + (g + 1) * HEAD_DIM]
            s = lax.dot_general(q4, k, (((1,), (1,)), ((), ())), preferred_element_type=F32)
            bias4 = jnp.concatenate([bias_ref[hh] for hh in heads], axis=0)
            logits = s * scale + bias4
            if j == 0:
                logits = jnp.where(col < jnp.where(i == 0, BLOCK, 0), -jnp.inf, logits)
            if j == nblk - 1:
                logits = jnp.where(col >= jnp.where(i == nt - 1, 2 * BLOCK, 3 * BLOCK), -jnp.inf, logits)
            sink = jnp.concatenate([jnp.full((BLOCK, 1), sink_ref[hh], F32) for hh in heads], axis=0)
            m = jnp.maximum(jnp.max(logits, axis=-1, keepdims=True), sink)
            p = jnp.exp(logits - m)
            denom = jnp.sum(p, axis=-1, keepdims=True) + jnp.exp(sink - m)
            o = jnp.dot(p.astype(BF16), v, preferred_element_type=F32) / denom
            for h, hh in enumerate(heads):
                o_ref[j * BLOCK:(j + 1) * BLOCK, hh * HEAD_DIM:(hh + 1) * HEAD_DIM] = (
                    o[h * BLOCK:(h + 1) * BLOCK].astype(BF16))


def _attention(q, kv, sink, bias_tab, b, s, tq):
    t, d = q.shape
    nt = s // tq
    nb = s // BLOCK
    r = tq // BLOCK
    return pl.pallas_call(
        functools.partial(_attn_kernel, tq),
        grid=(b, nt),
        in_specs=[pl.BlockSpec(memory_space=pltpu.SMEM),
                  pl.BlockSpec((tq, d), lambda bi, i: (bi * nt + i, 0)),
                  pl.BlockSpec((BLOCK, 2 * KV_DIM), lambda bi, i: (bi * nb + jnp.maximum(i * r - 1, 0), 0)),
                  pl.BlockSpec((tq, 2 * KV_DIM), lambda bi, i: (bi * nt + i, 0)),
                  pl.BlockSpec((BLOCK, 2 * KV_DIM), lambda bi, i: (bi * nb + jnp.minimum((i + 1) * r, nb - 1), 0)),
                  _const_spec(bias_tab.shape)],
        out_specs=pl.BlockSpec((tq, d), lambda bi, i: (bi * nt + i, 0)),
        out_shape=jax.ShapeDtypeStruct((t, d), BF16),
        scratch_shapes=[pltpu.VMEM((tq + 2 * BLOCK, 2 * KV_DIM), BF16)],
        compiler_params=_cparams(("parallel", "parallel")),
        name="attention",
    )(sink.astype(F32), q, kv, kv, kv, bias_tab)


def _gelu_tanh(x):
    return 0.5 * x * (1.0 + jnp.tanh(math.sqrt(2.0 / math.pi) * (x + 0.044715 * (x * x * x))))


def _softplus(x):
    return jnp.maximum(x, 0.0) + jnp.log1p(jnp.exp(-jnp.abs(x)))


def _lru_kernel(reverse, tm, xp_ref, xc_ref, xn_ref, cw_ref, cb_ref, wg_ref, br_ref, bi_ref, lam_ref, *rest):
    if reverse:
        hf_ref, gr_ref, o_ref, xe_scr, a_scr, b_scr, carry_scr = rest
    else:
        o_ref, xe_scr, a_scr, b_scr, carry_scr = rest
    i = pl.program_id(1)
    nt = pl.num_programs(1)
    t_idx = nt - 1 - i if reverse else i
    d = xc_ref.shape[1]
    halo = BF16_ROWS
    bd = d // LRU_BLOCKS

    xe_scr[0:halo] = jnp.where(t_idx > 0, xp_ref[...].astype(F32), 0.0)
    xe_scr[halo:halo + tm] = xc_ref[...].astype(F32)
    xe_scr[halo + tm:] = jnp.where(t_idx < nt - 1, xn_ref[...].astype(F32), 0.0)

    for n in range(LRU_BLOCKS):
        cs = slice(n * bd, (n + 1) * bd)
        xc = cb_ref[:, cs] + xe_scr[halo - CONV_LEFT:halo - CONV_LEFT + tm, cs] * cw_ref[0:1, cs]
        for j in range(1, CONV_WIDTH):
            xc = xc + xe_scr[halo - CONV_LEFT + j:halo - CONV_LEFT + j + tm, cs] * cw_ref[j:j + 1, cs]
        z = jnp.dot(xc.astype(BF16), wg_ref[n], preferred_element_type=F32)
        r = jax.nn.sigmoid(z[:, :bd] + br_ref[:, cs])
        ig = jax.nn.sigmoid(z[:, bd:] + bi_ref[:, cs])
        log_a = (-LRU_C) * r * _softplus(-lam_ref[:, cs])
        a = jnp.exp(log_a)
        mult = jnp.sqrt(-jnp.tanh(log_a) * (a * a + 1.0))
        a_scr[:, cs] = a
        b_scr[:, cs] = mult * (ig * xc)

    @pl.when(i == 0)
    def _():
        carry_scr[...] = jnp.zeros_like(carry_scr)

    row = lax.broadcasted_iota(jnp.int32, (SUBLANES, d), 0)
    ngroups = tm // SUBLANES

    def body(gi, carry):
        g_idx = ngroups - 1 - gi if reverse else gi
        rows = pl.ds(pl.multiple_of(g_idx * SUBLANES, SUBLANES), SUBLANES)
        av = a_scr[rows, :]
        bv = b_scr[rows, :]
        for sft in (1, 2, 4):
            if reverse:
                a_sh = pltpu.roll(av, SUBLANES - sft, axis=0)
                b_sh = pltpu.roll(bv, SUBLANES - sft, axis=0)
                msk = row < SUBLANES - sft
            else:
                a_sh = pltpu.roll(av, sft, axis=0)
                b_sh = pltpu.roll(bv, sft, axis=0)
                msk = row >= sft
            bv = jnp.where(msk, av * b_sh + bv, bv)
            av = jnp.where(msk, av * a_sh, av)
        hv = av * carry + bv
        b_scr[rows, :] = hv
        return hv[0:1, :] if reverse else hv[SUBLANES - 1:SUBLANES, :]

    carry_scr[...] = lax.fori_loop(0, ngroups, body, carry_scr[...])

    if reverse:
        y = (hf_ref[...].astype(F32) + b_scr[...]) * _gelu_tanh(gr_ref[...].astype(F32))
        o_ref[...] = y.astype(BF16)
    else:
        o_ref[...] = b_scr[...].astype(BF16)


def _lru(reverse, xr, conv_w, conv_b, wg, b_r, b_i, lam, extra, b, s, tm):
    t, d = xr.shape
    nt = s // tm
    nh = s // BF16_ROWS
    r = tm // BF16_ROWS

    def tix(i):
        return nt - 1 - i if reverse else i

    tile = pl.BlockSpec((tm, d), lambda bi, i: (bi * nt + tix(i), 0))
    in_specs = [
        pl.BlockSpec((BF16_ROWS, d), lambda bi, i: (bi * nh + jnp.maximum(tix(i) * r - 1, 0), 0)),
        tile,
        pl.BlockSpec((BF16_ROWS, d), lambda bi, i: (bi * nh + jnp.minimum((tix(i) + 1) * r, nh - 1), 0)),
        _const_spec((CONV_WIDTH, d)), _const_spec((1, d)), _const_spec(wg.shape),
        _const_spec((1, d)), _const_spec((1, d)), _const_spec((1, d)),
    ] + [tile] * len(extra)
    return pl.pallas_call(
        functools.partial(_lru_kernel, reverse, tm),
        grid=(b, nt),
        in_specs=in_specs,
        out_specs=tile,
        out_shape=jax.ShapeDtypeStruct((t, d), BF16),
        scratch_shapes=[pltpu.VMEM((tm + 2 * BF16_ROWS, d), F32),
                        pltpu.VMEM((tm, d), F32),
                        pltpu.VMEM((tm, d), F32),
                        pltpu.VMEM((1, d), F32)],
        compiler_params=_cparams(("parallel", "arbitrary")),
        name="lru_bwd" if reverse else "lru_fwd",
    )(xr, xr, xr, conv_w, conv_b.reshape(1, d), wg, b_r.reshape(1, d), b_i.reshape(1, d),
      lam.reshape(1, d), *extra)


def _mix_mlp_kernel(ff_chunk, final, x_ref, at_ref, y_ref, ga_ref, gb_ref, woa_ref, wol_ref, wout_ref,
                    g2_ref, wup_ref, wdn_ref, gf_ref, o_ref):
    ya = jnp.dot(at_ref[...], woa_ref[...], preferred_element_type=F32)
    yb = jnp.dot(y_ref[...], wol_ref[...], preferred_element_type=F32)
    mix = jax.nn.sigmoid(ga_ref[...].astype(F32)) * ya + jax.nn.sigmoid(gb_ref[...].astype(F32)) * yb
    x1 = x_ref[...] + jnp.dot(mix.astype(BF16), wout_ref[...], preferred_element_type=F32)
    h2 = _rms(x1, g2_ref[...]).astype(BF16)
    acc = x1
    dff = wup_ref.shape[1]
    for c0 in range(0, dff, ff_chunk):
        u = jnp.dot(h2, wup_ref[:, c0:c0 + ff_chunk], preferred_element_type=F32)
        u = jnp.square(jnp.maximum(u, 0.0)).astype(BF16)
        acc = acc + jnp.dot(u, wdn_ref[c0:c0 + ff_chunk, :], preferred_element_type=F32)
    if final:
        acc = _rms(acc, gf_ref[...])
    o_ref[...] = acc


def _mix_mlp(xf, attn, y, ga, gb, woa, wol, wout, g2, wup, wdn, gf, final, tm):
    t, d = xf.shape
    tile = pl.BlockSpec((tm, d), lambda i: (i, 0))
    return pl.pallas_call(
        functools.partial(_mix_mlp_kernel, min(1024, wup.shape[1]), final),
        grid=(t // tm,),
        in_specs=[tile] * 5 + [_const_spec(woa.shape), _const_spec(wol.shape), _const_spec(wout.shape),
                               _const_spec((1, d)), _const_spec(wup.shape), _const_spec(wdn.shape),
                               _const_spec((1, d))],
        out_specs=tile,
        out_shape=jax.ShapeDtypeStruct((t, d), F32),
        compiler_params=_cparams(("parallel",)),
        name="mix_mlp",
    )(xf, attn, y, ga, gb, woa, wol, wout, g2.reshape(1, d), wup, wdn, gf.reshape(1, d))


def _tile(n, pref):
    t = min(pref, n)
    assert n % t == 0, (n, t)
    return t


def kernel(x, w_in, w_o_attn, w_o_lru, w_out, attn_sink, rel_bias, conv_w, conv_b, w_rgate, b_rgate,
           w_igate, b_igate, lru_lambda, norm1_g, norm2_g, w_mlp_up, w_mlp_down, final_norm_g):
    b, s, d = x.shape
    depth = w_in.shape[0]
    assert d == N_HEADS * HEAD_DIM and s % BLOCK == 0
    t = b * s
    tm = _tile(t, 512)
    tq = _tile(s, 512)
    ts = _tile(s, 512)

    xf = x.reshape(t, d)
    bias_tab = _bias_table(rel_bias)
    for l in range(depth):
        q, kv, xr, gr, ga, gb = _inproj(xf, norm1_g[l], w_in[l].astype(BF16), tm)
        attn = _attention(q, kv, attn_sink[l], bias_tab, b, s, tq)
        wg = [jnp.concatenate([w_rgate[l, dr], w_igate[l, dr]], axis=-1).astype(BF16) for dr in range(2)]
        hf = _lru(False, xr, conv_w[l], conv_b[l], wg[0], b_rgate[l, 0], b_igate[l, 0], lru_lambda[l, 0],
                  (), b, s, ts)
        y = _lru(True, xr, conv_w[l], conv_b[l], wg[1], b_rgate[l, 1], b_igate[l, 1], lru_lambda[l, 1],
                 (hf, gr), b, s, ts)
        xf = _mix_mlp(xf, attn, y, ga, gb, w_o_attn[l].astype(BF16), w_o_lru[l].astype(BF16),
                      w_out[l].astype(BF16), norm2_g[l], w_mlp_up[l].astype(BF16), w_mlp_down[l].astype(BF16),
                      final_norm_g, l == depth - 1, tm)
    return xf.reshape(b, s, d)
```

```python
import functools
import math

import jax
import jax.numpy as jnp
from jax import lax
from jax.experimental import pallas as pl
from jax.experimental.pallas import tpu as pltpu

N_HEADS = 8
N_KV_HEADS = 2
HEAD_DIM = 128
GROUP = N_HEADS // N_KV_HEADS
KV_DIM = N_KV_HEADS * HEAD_DIM
WINDOW = 128
BLOCK = 128
NUM_BUCKETS = 32
MAX_DISTANCE = 128
LRU_BLOCKS = 8
LRU_C = 8.0
CONV_WIDTH = 4
CONV_LEFT = 2
CONV_RIGHT = CONV_WIDTH - 1 - CONV_LEFT
EPS = 1e-6

SUBLANES = 8
LANES = 128
BF16_ROWS = 16
VMEM_LIMIT = 56 * 1024 * 1024

F32 = jnp.float32
BF16 = jnp.bfloat16


def _cparams(sem):
    return pltpu.CompilerParams(dimension_semantics=sem, vmem_limit_bytes=VMEM_LIMIT)


def _const_spec(shape):
    nd = len(shape)
    return pl.BlockSpec(shape, lambda *_: (0,) * nd, pipeline_mode=pl.Buffered(1))


def _rms(xv, g):
    ms = jnp.mean(xv * xv, axis=-1, keepdims=True)
    return xv * lax.rsqrt(ms + EPS) * g


def _t5_bucket(rel):
    half = NUM_BUCKETS // 2
    max_exact = half // 2
    n = jnp.abs(rel)
    large = max_exact + (jnp.log(jnp.maximum(n, 1).astype(F32) / max_exact)
                         / math.log(MAX_DISTANCE / max_exact) * (half - max_exact)).astype(jnp.int32)
    large = jnp.minimum(large, half - 1)
    return jnp.where(rel > 0, half, 0) + jnp.where(n < max_exact, n, large)


def _bias_kernel(rb_ref, bucket_ref, o_ref):
    h = pl.program_id(0)
    bk = bucket_ref[...]
    acc = jnp.zeros(bk.shape, F32)
    for b in range(NUM_BUCKETS):
        acc = jnp.where(bk == b, rb_ref[b * N_HEADS + h], acc)
    j = lax.broadcasted_iota(jnp.int32, bk.shape, 0)
    q = lax.broadcasted_iota(jnp.int32, bk.shape, 1)
    rel = j - BLOCK - q
    o_ref[0] = jnp.where(jnp.abs(rel) <= WINDOW, acc, -jnp.inf)


def _bias_table(rel_bias):
    q_off = jnp.arange(BLOCK, dtype=jnp.int32)[None, :]
    j_off = jnp.arange(3 * BLOCK, dtype=jnp.int32)[:, None] - BLOCK
    bucket = _t5_bucket(j_off - q_off)
    return pl.pallas_call(
        _bias_kernel,
        grid=(N_HEADS,),
        in_specs=[pl.BlockSpec(memory_space=pltpu.SMEM),
                  pl.BlockSpec((3 * BLOCK, BLOCK), lambda h: (0, 0))],
        out_specs=pl.BlockSpec((1, 3 * BLOCK, BLOCK), lambda h: (h // GROUP, 0, h % GROUP)),
        out_shape=jax.ShapeDtypeStruct((N_KV_HEADS, 3 * BLOCK, GROUP * BLOCK), F32),
        name="bias_table",
    )(rel_bias.astype(F32).reshape(-1), bucket)


def _inproj_kernel(splits, x_ref, g_ref, w_ref, *out_refs):
    h = _rms(x_ref[...], g_ref[...]).astype(BF16)
    c0 = 0
    for idx, (o_ref, width) in enumerate(zip(out_refs, splits)):
        z = jnp.dot(h, w_ref[:, c0:c0 + width], preferred_element_type=F32)
        if idx == 0:
            z = z * (HEAD_DIM ** -0.5)
        o_ref[...] = z.astype(BF16)
        c0 += width


def _inproj(xf, g, w, tm):
    t, d = xf.shape
    splits = (d, 2 * KV_DIM, d, d, d, d)
    assert sum(splits) == w.shape[1]
    return pl.pallas_call(
        functools.partial(_inproj_kernel, splits),
        grid=(t // tm,),
        in_specs=[pl.BlockSpec((tm, d), lambda i: (i, 0)),
                  _const_spec((1, d)),
                  _const_spec(w.shape)],
        out_specs=[pl.BlockSpec((tm, c), lambda i: (i, 0)) for c in splits],
        out_shape=[jax.ShapeDtypeStruct((t, c), BF16) for c in splits],
        compiler_params=_cparams(("parallel",)),
        name="inproj",
    )(xf, g.reshape(1, d), w)


def _attn_kernel(tq, sink_ref, q_ref, kvp_ref, kvc_ref, kvn_ref, bias_ref, o_ref, kv_scr):
    i = pl.program_id(1)
    nt = pl.num_programs(1)
    nblk = tq // BLOCK
    kv_scr[0:BLOCK] = kvp_ref[...]
    kv_scr[BLOCK:BLOCK + tq] = kvc_ref[...]
    kv_scr[BLOCK + tq:] = kvn_ref[...]
    for j in range(nblk):
        for g in range(N_KV_HEADS):
            heads = [g * GROUP + h for h in range(GROUP)]
            q4 = jnp.concatenate(
                [q_ref[j * BLOCK:(j + 1) * BLOCK, hh * HEAD_DIM:(hh + 1) * HEAD_DIM] for hh in heads], axis=0)
            k = kv_scr[j * BLOCK:(j + 3) * BLOCK, g * HEAD_DIM:(g + 1) * HEAD_DIM]
            v = kv_scr[j * BLOCK:(j + 3) * BLOCK, KV_DIM + g * HEAD_DIM:KV_DIM + (g + 1) * HEAD_DIM]
            st = lax.dot_general(k, q4, (((1,), (1,)), ((), ())), preferred_element_type=F32)
            logits = st + bias_ref[g]
            if j == 0:
                top = jnp.where(i == 0, -jnp.inf, logits[:BLOCK])
                logits = jnp.concatenate([top, logits[BLOCK:]], axis=0)
            if j == nblk - 1:
                bot = jnp.where(i == nt - 1, -jnp.inf, logits[2 * BLOCK:])
                logits = jnp.concatenate([logits[:2 * BLOCK], bot], axis=0)
            sink = jnp.concatenate([jnp.full((1, BLOCK), sink_ref[hh], F32) for hh in heads], axis=1)
            m = jnp.maximum(jnp.max(logits, axis=0, keepdims=True), sink)
            p = jnp.exp(logits - m)
            denom = jnp.sum(p, axis=0, keepdims=True) + jnp.exp(sink - m)
            pn = (p * (1.0 / denom)).astype(BF16)
            o = lax.dot_general(pn, v, (((0,), (0,)), ((), ())), preferred_element_type=F32)
            for h, hh in enumerate(heads):
                o_ref[j * BLOCK:(j + 1) * BLOCK, hh * HEAD_DIM:(hh + 1) * HEAD_DIM] = (
                    o[h * BLOCK:(h + 1) * BLOCK].astype(BF16))


def _attention(q, kv, sink, bias_tab, b, s, tq):
    t, d = q.shape
    nt = s // tq
    nb = s // BLOCK
    r = tq // BLOCK
    return pl.pallas_call(
        functools.partial(_attn_kernel, tq),
        grid=(b, nt),
        in_specs=[pl.BlockSpec(memory_space=pltpu.SMEM),
                  pl.BlockSpec((tq, d), lambda bi, i: (bi * nt + i, 0)),
                  pl.BlockSpec((BLOCK, 2 * KV_DIM), lambda bi, i: (bi * nb + jnp.maximum(i * r - 1, 0), 0)),
                  pl.BlockSpec((tq, 2 * KV_DIM), lambda bi, i: (bi * nt + i, 0)),
                  pl.BlockSpec((BLOCK, 2 * KV_DIM), lambda bi, i: (bi * nb + jnp.minimum((i + 1) * r, nb - 1), 0)),
                  _const_spec(bias_tab.shape)],
        out_specs=pl.BlockSpec((tq, d), lambda bi, i: (bi * nt + i, 0)),
        out_shape=jax.ShapeDtypeStruct((t, d), BF16),
        scratch_shapes=[pltpu.VMEM((tq + 2 * BLOCK, 2 * KV_DIM), BF16)],
        compiler_params=_cparams(("parallel", "parallel")),
        name="attention",
    )(sink.astype(F32), q, kv, kv, kv, bias_tab)


def _gelu_tanh(x):
    return 0.5 * x * (1.0 + jnp.tanh(math.sqrt(2.0 / math.pi) * (x + 0.044715 * (x * x * x))))


def _softplus(x):
    return jnp.maximum(x, 0.0) + jnp.log1p(jnp.exp(-jnp.abs(x)))


def _lru_kernel(reverse, ts, xp_ref, xc_ref, xn_ref, cw_ref, cb_ref, wg_ref, br_ref, bi_ref, lam_ref, *rest):
    if reverse:
        hf_ref, gr_ref, o_ref, x_tm, a_tm, b_tm, carry_scr = rest
    else:
        o_ref, x_tm, a_tm, b_tm, carry_scr = rest
    i = pl.program_id(0)
    nt = pl.num_programs(0)
    t_idx = nt - 1 - i if reverse else i
    nb = xc_ref.shape[0]
    rows = ts * nb
    lo = CONV_LEFT * nb

    for b in range(nb):
        prev = jnp.where(t_idx > 0, xp_ref[b, BF16_ROWS - SUBLANES:, :].astype(F32), 0.0)
        nxt = jnp.where(t_idx < nt - 1, xn_ref[b, :SUBLANES, :].astype(F32), 0.0)
        for c in range(LRU_BLOCKS):
            cs = slice(c * LANES, (c + 1) * LANES)
            x_tm[c, pl.ds(lo + b, ts, stride=nb), :] = xc_ref[b, :, cs].astype(F32)
            x_tm[c, pl.ds(b, CONV_LEFT, stride=nb), :] = prev[SUBLANES - CONV_LEFT:, cs]
            x_tm[c, pl.ds(lo + rows + b, CONV_RIGHT, stride=nb), :] = nxt[:CONV_RIGHT, cs]

    for c in range(LRU_BLOCKS):
        cs = slice(c * LANES, (c + 1) * LANES)
        xc = cb_ref[:, cs] + x_tm[c, 0:rows, :] * cw_ref[0:1, cs]
        for j in range(1, CONV_WIDTH):
            xc = xc + x_tm[c, j * nb:j * nb + rows, :] * cw_ref[j:j + 1, cs]
        z = jnp.dot(xc.astype(BF16), wg_ref[c], preferred_element_type=F32)
        r = jax.nn.sigmoid(z[:, :LANES] + br_ref[:, cs])
        ig = jax.nn.sigmoid(z[:, LANES:] + bi_ref[:, cs])
        log_a = r * ((-LRU_C) * _softplus(-lam_ref[:, cs]))
        a = jnp.exp(log_a)
        y = -jnp.tanh(log_a) * (a * a + 1.0)
        mult = jnp.where(y > 0.0, y * lax.rsqrt(y), 0.0)
        a_tm[c] = a
        b_tm[c] = mult * (ig * xc)

    @pl.when(i == 0)
    def _():
        carry_scr[...] = jnp.zeros_like(carry_scr)

    def body(si, hs):
        s_idx = ts - 1 - si if reverse else si
        rr = pl.ds(pl.multiple_of(s_idx * nb, nb), nb)
        out = []
        for c in range(LRU_BLOCKS):
            hv = a_tm[c, rr, :] * hs[c] + b_tm[c, rr, :]
            b_tm[c, rr, :] = hv
            out.append(hv)
        return tuple(out)

    hs = lax.fori_loop(0, ts, body, tuple(carry_scr[c] for c in range(LRU_BLOCKS)), unroll=2)
    for c in range(LRU_BLOCKS):
        carry_scr[c] = hs[c]

    for b in range(nb):
        for c in range(LRU_BLOCKS):
            cs = slice(c * LANES, (c + 1) * LANES)
            hv = b_tm[c, pl.ds(b, ts, stride=nb), :]
            if reverse:
                hv = (hf_ref[b, :, cs].astype(F32) + hv) * _gelu_tanh(gr_ref[b, :, cs].astype(F32))
            o_ref[b, :, cs] = hv.astype(BF16)


def _lru(reverse, xr, conv_w, conv_b, wg, b_r, b_i, lam, extra, b, s, ts):
    t, d = xr.shape
    assert b == SUBLANES and d == LRU_BLOCKS * LANES
    nt = s // ts
    nh = s // BF16_ROWS
    r = ts // BF16_ROWS

    def tix(i):
        return nt - 1 - i if reverse else i

    x3 = xr.reshape(b, s, d)
    tile = pl.BlockSpec((b, ts, d), lambda i: (0, tix(i), 0))
    in_specs = [
        pl.BlockSpec((b, BF16_ROWS, d), lambda i: (0, jnp.maximum(tix(i) * r - 1, 0), 0)),
        tile,
        pl.BlockSpec((b, BF16_ROWS, d), lambda i: (0, jnp.minimum((tix(i) + 1) * r, nh - 1), 0)),
        _const_spec((CONV_WIDTH, d)), _const_spec((1, d)), _const_spec(wg.shape),
        _const_spec((1, d)), _const_spec((1, d)), _const_spec((1, d)),
    ] + [tile] * len(extra)
    rows = ts * b
    out = pl.pallas_call(
        functools.partial(_lru_kernel, reverse, ts),
        grid=(nt,),
        in_specs=in_specs,
        out_specs=tile,
        out_shape=jax.ShapeDtypeStruct((b, s, d), BF16),
        scratch_shapes=[pltpu.VMEM((LRU_BLOCKS, rows + (CONV_WIDTH - 1) * b, LANES), F32),
                        pltpu.VMEM((LRU_BLOCKS, rows, LANES), F32),
                        pltpu.VMEM((LRU_BLOCKS, rows, LANES), F32),
                        pltpu.VMEM((LRU_BLOCKS, b, LANES), F32)],
        compiler_params=_cparams(("arbitrary",)),
        name="lru_bwd" if reverse else "lru_fwd",
    )(x3, x3, x3, conv_w, conv_b.reshape(1, d), wg, b_r.reshape(1, d), b_i.reshape(1, d),
      lam.reshape(1, d), *[e.reshape(b, s, d) for e in extra])
    return out.reshape(t, d)


def _mix_mlp_kernel(ff_chunk, final, x_ref, at_ref, y_ref, ga_ref, gb_ref, woa_ref, wol_ref, wout_ref,
                    g2_ref, wup_ref, wdn_ref, gf_ref, o_ref):
    ya = jnp.dot(at_ref[...], woa_ref[...], preferred_element_type=F32)
    yb = jnp.dot(y_ref[...], wol_ref[...], preferred_element_type=F32)
    mix = jax.nn.sigmoid(ga_ref[...].astype(F32)) * ya + jax.nn.sigmoid(gb_ref[...].astype(F32)) * yb
    x1 = x_ref[...] + jnp.dot(mix.astype(BF16), wout_ref[...], preferred_element_type=F32)
    h2 = _rms(x1, g2_ref[...]).astype(BF16)
    acc = x1
    dff = wup_ref.shape[1]
    for c0 in range(0, dff, ff_chunk):
        u = jnp.dot(h2, wup_ref[:, c0:c0 + ff_chunk], preferred_element_type=F32)
        u = jnp.square(jnp.maximum(u, 0.0)).astype(BF16)
        acc = acc + jnp.dot(u, wdn_ref[c0:c0 + ff_chunk, :], preferred_element_type=F32)
    if final:
        acc = _rms(acc, gf_ref[...])
    o_ref[...] = acc


def _mix_mlp(xf, attn, y, ga, gb, woa, wol, wout, g2, wup, wdn, gf, final, tm):
    t, d = xf.shape
    tile = pl.BlockSpec((tm, d), lambda i: (i, 0))
    return pl.pallas_call(
        functools.partial(_mix_mlp_kernel, min(1024, wup.shape[1]), final),
        grid=(t // tm,),
        in_specs=[tile] * 5 + [_const_spec(woa.shape), _const_spec(wol.shape), _const_spec(wout.shape),
                               _const_spec((1, d)), _const_spec(wup.shape), _const_spec(wdn.shape),
                               _const_spec((1, d))],
        out_specs=tile,
        out_shape=jax.ShapeDtypeStruct((t, d), F32),
        compiler_params=_cparams(("parallel",)),
        name="mix_mlp",
    )(xf, attn, y, ga, gb, woa, wol, wout, g2.reshape(1, d), wup, wdn, gf.reshape(1, d))


def _tile(n, pref):
    t = min(pref, n)
    assert n % t == 0, (n, t)
    return t


def kernel(x, w_in, w_o_attn, w_o_lru, w_out, attn_sink, rel_bias, conv_w, conv_b, w_rgate, b_rgate,
           w_igate, b_igate, lru_lambda, norm1_g, norm2_g, w_mlp_up, w_mlp_down, final_norm_g):
    b, s, d = x.shape
    depth = w_in.shape[0]
    assert d == N_HEADS * HEAD_DIM and s % BLOCK == 0
    t = b * s
    tm = _tile(t, 512)
    tq = _tile(s, 512)
    ts = _tile(s, 64)

    xf = x.reshape(t, d)
    bias_tab = _bias_table(rel_bias)
    for l in range(depth):
        q, kv, xr, gr, ga, gb = _inproj(xf, norm1_g[l], w_in[l].astype(BF16), tm)
        attn = _attention(q, kv, attn_sink[l], bias_tab, b, s, tq)
        wg = [jnp.concatenate([w_rgate[l, dr], w_igate[l, dr]], axis=-1).astype(BF16) for dr in range(2)]
        hf = _lru(False, xr, conv_w[l], conv_b[l], wg[0], b_rgate[l, 0], b_igate[l, 0], lru_lambda[l, 0],
                  (), b, s, ts)
        y = _lru(True, xr, conv_w[l], conv_b[l], wg[1], b_rgate[l, 1], b_igate[l, 1], lru_lambda[l, 1],
                 (hf, gr), b, s, ts)
        xf = _mix_mlp(xf, attn, y, ga, gb, w_o_attn[l].astype(BF16), w_o_lru[l].astype(BF16),
                      w_out[l].astype(BF16), norm2_g[l], w_mlp_up[l].astype(BF16), w_mlp_down[l].astype(BF16),
                      final_norm_g, l == depth - 1, tm)
    return xf.reshape(b, s, d)
```

```python
import functools
import math

import jax
import jax.numpy as jnp
from jax import lax
from jax.experimental import pallas as pl
from jax.experimental.pallas import tpu as pltpu

N_HEADS = 8
N_KV_HEADS = 2
HEAD_DIM = 128
GROUP = N_HEADS // N_KV_HEADS
KV_DIM = N_KV_HEADS * HEAD_DIM
WINDOW = 128
BLOCK = 128
NUM_BUCKETS = 32
MAX_DISTANCE = 128
LRU_BLOCKS = 8
LRU_C = 8.0
CONV_WIDTH = 4
CONV_LEFT = 2
CONV_RIGHT = CONV_WIDTH - 1 - CONV_LEFT
EPS = 1e-6

SUBLANES = 8
LANES = 128
BF16_ROWS = 16
PROJ_CHUNK = 512
VMEM_LIMIT = 56 * 1024 * 1024

F32 = jnp.float32
BF16 = jnp.bfloat16


def _cparams(sem):
    return pltpu.CompilerParams(dimension_semantics=sem, vmem_limit_bytes=VMEM_LIMIT)


def _const_spec(shape):
    nd = len(shape)
    return pl.BlockSpec(shape, lambda *_: (0,) * nd, pipeline_mode=pl.Buffered(1))


def _rms(xv, g):
    ms = jnp.mean(xv * xv, axis=-1, keepdims=True)
    return xv * lax.rsqrt(ms + EPS) * g


def _t5_bucket(rel):
    half = NUM_BUCKETS // 2
    max_exact = half // 2
    n = jnp.abs(rel)
    large = max_exact + (jnp.log(jnp.maximum(n, 1).astype(F32) / max_exact)
                         / math.log(MAX_DISTANCE / max_exact) * (half - max_exact)).astype(jnp.int32)
    large = jnp.minimum(large, half - 1)
    return jnp.where(rel > 0, half, 0) + jnp.where(n < max_exact, n, large)


def _bias_kernel(rb_ref, bucket_ref, o_ref):
    h = pl.program_id(0)
    bk = bucket_ref[...]
    acc = jnp.zeros(bk.shape, F32)
    for b in range(NUM_BUCKETS):
        acc = jnp.where(bk == b, rb_ref[b * N_HEADS + h], acc)
    j = lax.broadcasted_iota(jnp.int32, bk.shape, 0)
    q = lax.broadcasted_iota(jnp.int32, bk.shape, 1)
    rel = j - BLOCK - q
    o_ref[0] = jnp.where(jnp.abs(rel) <= WINDOW, acc, -jnp.inf)


def _bias_table(rel_bias):
    q_off = jnp.arange(BLOCK, dtype=jnp.int32)[None, :]
    j_off = jnp.arange(3 * BLOCK, dtype=jnp.int32)[:, None] - BLOCK
    bucket = _t5_bucket(j_off - q_off)
    return pl.pallas_call(
        _bias_kernel,
        grid=(N_HEADS,),
        in_specs=[pl.BlockSpec(memory_space=pltpu.SMEM),
                  pl.BlockSpec((3 * BLOCK, BLOCK), lambda h: (0, 0))],
        out_specs=pl.BlockSpec((1, 3 * BLOCK, BLOCK), lambda h: (h // GROUP, 0, h % GROUP)),
        out_shape=jax.ShapeDtypeStruct((N_KV_HEADS, 3 * BLOCK, GROUP * BLOCK), F32),
        name="bias_table",
    )(rel_bias.astype(F32).reshape(-1), bucket)


def _gelu_tanh(x):
    return 0.5 * x * (1.0 + jnp.tanh(math.sqrt(2.0 / math.pi) * (x + 0.044715 * (x * x * x))))


def _softplus(x):
    return jnp.maximum(x, 0.0) + jnp.log1p(jnp.exp(-jnp.abs(x)))


def _lru_core(reverse, ts, nb, x_tm, a_tm, b_tm, carry_scr,
              cw_ref, cb_ref, wg_ref, br_ref, bi_ref, lam_ref, between_blocks):
    rows = ts * nb
    for c in range(LRU_BLOCKS):
        cs = slice(c * LANES, (c + 1) * LANES)
        xc = cb_ref[:, cs] + x_tm[c, 0:rows, :] * cw_ref[0:1, cs]
        for j in range(1, CONV_WIDTH):
            xc = xc + x_tm[c, j * nb:j * nb + rows, :] * cw_ref[j:j + 1, cs]
        z = jnp.dot(xc.astype(BF16), wg_ref[c], preferred_element_type=F32)
        r = jax.nn.sigmoid(z[:, :LANES] + br_ref[:, cs])
        ig = jax.nn.sigmoid(z[:, LANES:] + bi_ref[:, cs])
        log_a = r * ((-LRU_C) * _softplus(-lam_ref[:, cs]))
        a = jnp.exp(log_a)
        y = -jnp.tanh(log_a) * (a * a + 1.0)
        mult = jnp.where(y > 0.0, y * lax.rsqrt(y), 0.0)
        a_tm[c] = a
        b_tm[c] = mult * (ig * xc)
        between_blocks(c)

    hs = [carry_scr[c] for c in range(LRU_BLOCKS)]
    for si in range(ts):
        s_idx = ts - 1 - si if reverse else si
        rr = slice(s_idx * nb, (s_idx + 1) * nb)
        for c in range(LRU_BLOCKS):
            hs[c] = a_tm[c, rr, :] * hs[c] + b_tm[c, rr, :]
            b_tm[c, rr, :] = hs[c]
    for c in range(LRU_BLOCKS):
        carry_scr[c] = hs[c]


def _lru_scratch(ts, nb):
    rows = ts * nb
    return [pltpu.VMEM((LRU_BLOCKS, rows + (CONV_WIDTH - 1) * nb, LANES), F32),
            pltpu.VMEM((LRU_BLOCKS, rows, LANES), F32),
            pltpu.VMEM((LRU_BLOCKS, rows, LANES), F32),
            pltpu.VMEM((LRU_BLOCKS, nb, LANES), F32)]


def _front_kernel(ts, x_ref, xh_ref, g_ref, w_ref, cw_ref, cb_ref, wg_ref, br_ref, bi_ref, lam_ref,
                  q_ref, kv_ref, xr_ref, gr_ref, ga_ref, gb_ref, hf_ref, x_tm, a_tm, b_tm, carry_scr):
    i = pl.program_id(0)
    nt = pl.num_programs(0)
    nb, _, d = x_ref.shape
    rows = ts * nb
    lo = CONV_LEFT * nb
    g = g_ref[...]

    @pl.when(i == 0)
    def _():
        carry_scr[...] = jnp.zeros_like(carry_scr)
        x_tm[:, rows:rows + lo, :] = jnp.zeros((LRU_BLOCKS, lo, LANES), F32)

    xv = jnp.concatenate([x_ref[b] for b in range(nb)] + [xh_ref[:, 0, :], xh_ref[:, 1, :]], axis=0)
    h_all = _rms(xv, g).astype(BF16)
    h = h_all[:rows]

    c_xr = d + 2 * KV_DIM
    z_xr = jnp.dot(h_all, w_ref[:, c_xr:c_xr + d], preferred_element_type=F32)
    for c in range(LRU_BLOCKS):
        x_tm[c, 0:lo, :] = x_tm[c, rows:rows + lo, :]
    for c in range(LRU_BLOCKS):
        cs = slice(c * LANES, (c + 1) * LANES)
        for b in range(nb):
            x_tm[c, pl.ds(lo + b, ts, stride=nb), :] = z_xr[b * ts:(b + 1) * ts, cs]
        x_tm[c, lo + rows:lo + rows + nb, :] = jnp.where(i < nt - 1, z_xr[rows:rows + nb, cs], 0.0)
    for b in range(nb):
        xr_ref[b] = z_xr[b * ts:(b + 1) * ts].astype(BF16)

    outs = ((q_ref, 0, d, HEAD_DIM ** -0.5), (kv_ref, d, 2 * KV_DIM, None),
            (gr_ref, c_xr + d, d, None), (ga_ref, c_xr + 2 * d, d, None), (gb_ref, c_xr + 3 * d, d, None))
    chunks = [(o_ref, c0, o0, scale) for o_ref, c0, width, scale in outs
              for o0 in range(0, width, PROJ_CHUNK)]

    def project(o_ref, c0, o0, scale):
        z = jnp.dot(h, w_ref[:, c0 + o0:c0 + o0 + PROJ_CHUNK], preferred_element_type=F32)
        if scale is not None:
            z = z * scale
        z = z.astype(BF16)
        for b in range(nb):
            o_ref[b, :, o0:o0 + PROJ_CHUNK] = z[b * ts:(b + 1) * ts]

    def between_blocks(c):
        for ch in (chunks[c:] if c == LRU_BLOCKS - 1 else chunks[c:c + 1]):
            project(*ch)

    _lru_core(False, ts, nb, x_tm, a_tm, b_tm, carry_scr,
              cw_ref, cb_ref, wg_ref, br_ref, bi_ref, lam_ref, between_blocks)

    for b in range(nb):
        for c in range(LRU_BLOCKS):
            cs = slice(c * LANES, (c + 1) * LANES)
            hf_ref[b, :, cs] = b_tm[c, pl.ds(b, ts, stride=nb), :].astype(BF16)


def _front(x3, g, w, conv_w, conv_b, wg, b_r, b_i, lam, ts):
    b, s, d = x3.shape
    assert b == SUBLANES and d == LRU_BLOCKS * LANES and w.shape[1] == 5 * d + 2 * KV_DIM
    nt = s // ts
    nh = s // SUBLANES
    r = ts // SUBLANES
    tile = lambda c: pl.BlockSpec((b, ts, c), lambda i: (0, i, 0))
    widths = (d, 2 * KV_DIM, d, d, d, d, d)
    return pl.pallas_call(
        functools.partial(_front_kernel, ts),
        grid=(nt,),
        in_specs=[tile(d),
                  pl.BlockSpec((b, SUBLANES, d), lambda i: (0, jnp.minimum((i + 1) * r, nh - 1), 0)),
                  _const_spec((1, d)), _const_spec(w.shape),
                  _const_spec((CONV_WIDTH, d)), _const_spec((1, d)), _const_spec(wg.shape),
                  _const_spec((1, d)), _const_spec((1, d)), _const_spec((1, d))],
        out_specs=[tile(c) for c in widths],
        out_shape=[jax.ShapeDtypeStruct((b, s, c), BF16) for c in widths],
        scratch_shapes=_lru_scratch(ts, b),
        compiler_params=_cparams(("arbitrary",)),
        name="front",
    )(x3, x3, g.reshape(1, d), w, conv_w, conv_b.reshape(1, d), wg, b_r.reshape(1, d), b_i.reshape(1, d),
      lam.reshape(1, d))


def _attn_kernel(tq, sink_ref, q_ref, kvp_ref, kvc_ref, kvn_ref, bias_ref, o_ref, kv_scr):
    i = pl.program_id(1)
    nt = pl.num_programs(1)
    nblk = tq // BLOCK
    kv_scr[0:BLOCK] = kvp_ref[...]
    kv_scr[BLOCK:BLOCK + tq] = kvc_ref[...]
    kv_scr[BLOCK + tq:] = kvn_ref[...]
    for j in range(nblk):
        for g in range(N_KV_HEADS):
            heads = [g * GROUP + h for h in range(GROUP)]
            q4 = jnp.concatenate(
                [q_ref[j * BLOCK:(j + 1) * BLOCK, hh * HEAD_DIM:(hh + 1) * HEAD_DIM] for hh in heads], axis=0)
            k = kv_scr[j * BLOCK:(j + 3) * BLOCK, g * HEAD_DIM:(g + 1) * HEAD_DIM]
            v = kv_scr[j * BLOCK:(j + 3) * BLOCK, KV_DIM + g * HEAD_DIM:KV_DIM + (g + 1) * HEAD_DIM]
            st = lax.dot_general(k, q4, (((1,), (1,)), ((), ())), preferred_element_type=F32)
            logits = st + bias_ref[g]
            if j == 0:
                top = jnp.where(i == 0, -jnp.inf, logits[:BLOCK])
                logits = jnp.concatenate([top, logits[BLOCK:]], axis=0)
            if j == nblk - 1:
                bot = jnp.where(i == nt - 1, -jnp.inf, logits[2 * BLOCK:])
                logits = jnp.concatenate([logits[:2 * BLOCK], bot], axis=0)
            sink = jnp.concatenate([jnp.full((1, BLOCK), sink_ref[hh], F32) for hh in heads], axis=1)
            m = jnp.maximum(jnp.max(logits, axis=0, keepdims=True), sink)
            p = jnp.exp(logits - m)
            denom = jnp.sum(p, axis=0, keepdims=True) + jnp.exp(sink - m)
            pn = (p * (1.0 / denom)).astype(BF16)
            o = lax.dot_general(pn, v, (((0,), (0,)), ((), ())), preferred_element_type=F32)
            for h, hh in enumerate(heads):
                o_ref[j * BLOCK:(j + 1) * BLOCK, hh * HEAD_DIM:(hh + 1) * HEAD_DIM] = (
                    o[h * BLOCK:(h + 1) * BLOCK].astype(BF16))


def _attention(q, kv, sink, bias_tab, b, s, tq):
    t, d = q.shape
    nt = s // tq
    nb = s // BLOCK
    r = tq // BLOCK
    return pl.pallas_call(
        functools.partial(_attn_kernel, tq),
        grid=(b, nt),
        in_specs=[pl.BlockSpec(memory_space=pltpu.SMEM),
                  pl.BlockSpec((tq, d), lambda bi, i: (bi * nt + i, 0)),
                  pl.BlockSpec((BLOCK, 2 * KV_DIM), lambda bi, i: (bi * nb + jnp.maximum(i * r - 1, 0), 0)),
                  pl.BlockSpec((tq, 2 * KV_DIM), lambda bi, i: (bi * nt + i, 0)),
                  pl.BlockSpec((BLOCK, 2 * KV_DIM), lambda bi, i: (bi * nb + jnp.minimum((i + 1) * r, nb - 1), 0)),
                  _const_spec(bias_tab.shape)],
        out_specs=pl.BlockSpec((tq, d), lambda bi, i: (bi * nt + i, 0)),
        out_shape=jax.ShapeDtypeStruct((t, d), BF16),
        scratch_shapes=[pltpu.VMEM((tq + 2 * BLOCK, 2 * KV_DIM), BF16)],
        compiler_params=_cparams(("parallel", "parallel")),
        name="attention",
    )(sink.astype(F32), q, kv, kv, kv, bias_tab)


def _back_kernel(ts, x_ref, at_ref, ga_ref, gb_ref, xp_ref, xc_ref, xn_ref, hf_ref, gr_ref,
                 cw_ref, cb_ref, wg_ref, br_ref, bi_ref, lam_ref, woa_ref, wol_ref, wout_ref,
                 o_ref, x_tm, a_tm, b_tm, carry_scr, y_scr):
    i = pl.program_id(0)
    nt = pl.num_programs(0) - 1
    t_idx = jnp.maximum(nt - 1 - i, 0)
    nb, _, d = x_ref.shape
    rows = ts * nb
    lo = CONV_LEFT * nb

    @pl.when(i == 0)
    def _():
        carry_scr[...] = jnp.zeros_like(carry_scr)
        y_scr[...] = jnp.zeros_like(y_scr)

    for b in range(nb):
        prev = jnp.where(t_idx > 0, xp_ref[b, BF16_ROWS - SUBLANES:, :].astype(F32), 0.0)
        nxt = jnp.where(t_idx < nt - 1, xn_ref[b, :SUBLANES, :].astype(F32), 0.0)
        for c in range(LRU_BLOCKS):
            cs = slice(c * LANES, (c + 1) * LANES)
            x_tm[c, pl.ds(lo + b, ts, stride=nb), :] = xc_ref[b, :, cs].astype(F32)
            x_tm[c, pl.ds(b, CONV_LEFT, stride=nb), :] = prev[SUBLANES - CONV_LEFT:, cs]
            x_tm[c, pl.ds(lo + rows + b, CONV_RIGHT, stride=nb), :] = nxt[:CONV_RIGHT, cs]

    attn = jnp.concatenate([at_ref[b] for b in range(nb)], axis=0)
    half = LRU_BLOCKS // 2
    cw_ = d // half
    ya_parts, yb_parts = [], []

    def between_blocks(c):
        k = c % half
        cols = slice(k * cw_, (k + 1) * cw_)
        if c < half:
            ya_parts.append(jnp.dot(attn, woa_ref[:, cols], preferred_element_type=F32))
        else:
            yb_parts.append(jnp.dot(y_scr[...], wol_ref[:, cols], preferred_element_type=F32))

    _lru_core(True, ts, nb, x_tm, a_tm, b_tm, carry_scr,
              cw_ref, cb_ref, wg_ref, br_ref, bi_ref, lam_ref, between_blocks)

    ya = jnp.concatenate(ya_parts, axis=1)
    yb = jnp.concatenate(yb_parts, axis=1)
    ga = jnp.concatenate([ga_ref[b] for b in range(nb)], axis=0).astype(F32)
    gb = jnp.concatenate([gb_ref[b] for b in range(nb)], axis=0).astype(F32)
    mix = jax.nn.sigmoid(ga) * ya + jax.nn.sigmoid(gb) * yb
    upd = jnp.dot(mix.astype(BF16), wout_ref[...], preferred_element_type=F32)
    for b in range(nb):
        o_ref[b] = x_ref[b] + upd[b * ts:(b + 1) * ts]

    for b in range(nb):
        for c in range(LRU_BLOCKS):
            cs = slice(c * LANES, (c + 1) * LANES)
            hb = b_tm[c, pl.ds(b, ts, stride=nb), :]
            yv = (hf_ref[b, :, cs].astype(F32) + hb) * _gelu_tanh(gr_ref[b, :, cs].astype(F32))
            y_scr[b * ts:(b + 1) * ts, cs] = yv.astype(BF16)


def _back(x3, attn3, ga3, gb3, xr3, hf3, gr3, conv_w, conv_b, wg, b_r, b_i, lam, woa, wol, wout, ts):
    b, s, d = x3.shape
    nt = s // ts
    nh = s // BF16_ROWS
    r = ts // BF16_ROWS
    tix = lambda i: jnp.maximum(nt - 1 - i, 0)
    mix_tix = lambda i: jnp.minimum(nt - i, nt - 1)
    tile = pl.BlockSpec((b, ts, d), lambda i: (0, tix(i), 0))
    mix_tile = pl.BlockSpec((b, ts, d), lambda i: (0, mix_tix(i), 0))
    return pl.pallas_call(
        functools.partial(_back_kernel, ts),
        grid=(nt + 1,),
        in_specs=[mix_tile, mix_tile, mix_tile, mix_tile,
                  pl.BlockSpec((b, BF16_ROWS, d), lambda i: (0, jnp.maximum(tix(i) * r - 1, 0), 0)),
                  tile,
                  pl.BlockSpec((b, BF16_ROWS, d), lambda i: (0, jnp.minimum((tix(i) + 1) * r, nh - 1), 0)),
                  tile, tile,
                  _const_spec((CONV_WIDTH, d)), _const_spec((1, d)), _const_spec(wg.shape),
                  _const_spec((1, d)), _const_spec((1, d)), _const_spec((1, d)),
                  _const_spec(woa.shape), _const_spec(wol.shape), _const_spec(wout.shape)],
        out_specs=mix_tile,
        out_shape=jax.ShapeDtypeStruct((b, s, d), F32),
        scratch_shapes=_lru_scratch(ts, b) + [pltpu.VMEM((ts * b, d), BF16)],
        compiler_params=_cparams(("arbitrary",)),
        name="back",
    )(x3, attn3, ga3, gb3, xr3, xr3, xr3, hf3, gr3, conv_w, conv_b.reshape(1, d), wg,
      b_r.reshape(1, d), b_i.reshape(1, d), lam.reshape(1, d), woa, wol, wout)


def _mlp_kernel(ff_chunk, final, x_ref, g2_ref, wup_ref, wdn_ref, gf_ref, o_ref):
    x1 = x_ref[...]
    h2 = _rms(x1, g2_ref[...]).astype(BF16)
    acc = x1
    dff = wup_ref.shape[1]
    for c0 in range(0, dff, ff_chunk):
        u = jnp.dot(h2, wup_ref[:, c0:c0 + ff_chunk], preferred_element_type=F32)
        u = jnp.square(jnp.maximum(u, 0.0)).astype(BF16)
        acc = acc + jnp.dot(u, wdn_ref[c0:c0 + ff_chunk, :], preferred_element_type=F32)
    if final:
        acc = _rms(acc, gf_ref[...])
    o_ref[...] = acc


def _mlp(xf, g2, wup, wdn, gf, final, tm):
    t, d = xf.shape
    tile = pl.BlockSpec((tm, d), lambda i: (i, 0))
    return pl.pallas_call(
        functools.partial(_mlp_kernel, min(1024, wup.shape[1]), final),
        grid=(t // tm,),
        in_specs=[tile, _const_spec((1, d)), _const_spec(wup.shape), _const_spec(wdn.shape),
                  _const_spec((1, d))],
        out_specs=tile,
        out_shape=jax.ShapeDtypeStruct((t, d), F32),
        compiler_params=_cparams(("parallel",)),
        name="mlp",
    )(xf, g2.reshape(1, d), wup, wdn, gf.reshape(1, d))


def _tile(n, pref):
    t = min(pref, n)
    assert n % t == 0, (n, t)
    return t


def kernel(x, w_in, w_o_attn, w_o_lru, w_out, attn_sink, rel_bias, conv_w, conv_b, w_rgate, b_rgate,
           w_igate, b_igate, lru_lambda, norm1_g, norm2_g, w_mlp_up, w_mlp_down, final_norm_g):
    b, s, d = x.shape
    depth = w_in.shape[0]
    assert d == N_HEADS * HEAD_DIM and s % BLOCK == 0
    t = b * s
    tm = _tile(t, 512)
    tq = _tile(s, 512)
    ts = _tile(s, 64)

    x3 = x
    bias_tab = _bias_table(rel_bias)
    for l in range(depth):
        wg = [jnp.concatenate([w_rgate[l, dr], w_igate[l, dr]], axis=-1).astype(BF16) for dr in range(2)]
        q, kv, xr, gr, ga, gb, hf = _front(x3, norm1_g[l], w_in[l].astype(BF16), conv_w[l], conv_b[l], wg[0],
                                           b_rgate[l, 0], b_igate[l, 0], lru_lambda[l, 0], ts)
        attn = _attention(q.reshape(t, d), kv.reshape(t, 2 * KV_DIM), attn_sink[l], bias_tab, b, s, tq)
        x1 = _back(x3, attn.reshape(b, s, d), ga, gb, xr, hf, gr, conv_w[l], conv_b[l], wg[1],
                   b_rgate[l, 1], b_igate[l, 1], lru_lambda[l, 1], w_o_attn[l].astype(BF16),
                   w_o_lru[l].astype(BF16), w_out[l].astype(BF16), ts)
        x3 = _mlp(x1.reshape(t, d), norm2_g[l], w_mlp_up[l].astype(BF16), w_mlp_down[l].astype(BF16),
                  final_norm_g, l == depth - 1, tm).reshape(b, s, d)
    return x3
```

```python
import functools
import math

import jax
import jax.numpy as jnp
from jax import lax
from jax.experimental import pallas as pl
from jax.experimental.pallas import tpu as pltpu

N_HEADS = 8
N_KV_HEADS = 2
HEAD_DIM = 128
GROUP = N_HEADS // N_KV_HEADS
KV_DIM = N_KV_HEADS * HEAD_DIM
WINDOW = 128
BLOCK = 128
NUM_BUCKETS = 32
MAX_DISTANCE = 128
LRU_BLOCKS = 8
LRU_C = 8.0
CONV_WIDTH = 4
CONV_LEFT = 2
CONV_RIGHT = CONV_WIDTH - 1 - CONV_LEFT
EPS = 1e-6

SUBLANES = 8
LANES = 128
BF16_ROWS = 16
PROJ_CHUNK = 512
LRU_TIME_CHUNKS = 1
ATTN_HEADS_PER_SLAB = 1
VMEM_LIMIT = 56 * 1024 * 1024

F32 = jnp.float32
BF16 = jnp.bfloat16


def _cparams(sem, flags=None):
    return pltpu.CompilerParams(dimension_semantics=sem, vmem_limit_bytes=VMEM_LIMIT, flags=flags)


def _const_spec(shape):
    nd = len(shape)
    return pl.BlockSpec(shape, lambda *_: (0,) * nd, pipeline_mode=pl.Buffered(1))


def _rms(xv, g):
    ms = jnp.mean(xv * xv, axis=-1, keepdims=True)
    return xv * lax.rsqrt(ms + EPS) * g


def _t5_bucket(rel):
    half = NUM_BUCKETS // 2
    max_exact = half // 2
    n = jnp.abs(rel)
    large = max_exact + (jnp.log(jnp.maximum(n, 1).astype(F32) / max_exact)
                         / math.log(MAX_DISTANCE / max_exact) * (half - max_exact)).astype(jnp.int32)
    large = jnp.minimum(large, half - 1)
    return jnp.where(rel > 0, half, 0) + jnp.where(n < max_exact, n, large)


def _bias_kernel(rb_ref, bucket_ref, o_ref):
    h = pl.program_id(0)
    bk = bucket_ref[...]
    acc = jnp.zeros(bk.shape, F32)
    for b in range(NUM_BUCKETS):
        acc = jnp.where(bk == b, rb_ref[b * N_HEADS + h], acc)
    j = lax.broadcasted_iota(jnp.int32, bk.shape, 0)
    q = lax.broadcasted_iota(jnp.int32, bk.shape, 1)
    rel = j - BLOCK - q
    o_ref[0] = jnp.where(jnp.abs(rel) <= WINDOW, acc, -jnp.inf)


def _bias_table(rel_bias):
    q_off = jnp.arange(BLOCK, dtype=jnp.int32)[None, :]
    j_off = jnp.arange(3 * BLOCK, dtype=jnp.int32)[:, None] - BLOCK
    bucket = _t5_bucket(j_off - q_off)
    return pl.pallas_call(
        _bias_kernel,
        grid=(N_HEADS,),
        in_specs=[pl.BlockSpec(memory_space=pltpu.SMEM),
                  pl.BlockSpec((3 * BLOCK, BLOCK), lambda h: (0, 0))],
        out_specs=pl.BlockSpec((1, 3 * BLOCK, BLOCK), lambda h: (h // GROUP, 0, h % GROUP)),
        out_shape=jax.ShapeDtypeStruct((N_KV_HEADS, 3 * BLOCK, GROUP * BLOCK), F32),
        name="bias_table",
    )(rel_bias.astype(F32).reshape(-1), bucket)


def _gelu_tanh(x):
    k = math.sqrt(2.0 / math.pi)
    hx = 0.5 * x
    return hx + hx * jnp.tanh(x * (k + (k * 0.044715) * (x * x)))


def _softplus(x):
    return jnp.maximum(x, 0.0) + jnp.log1p(jnp.exp(-jnp.abs(x)))


def _lru_core(reverse, ts, nb, x_tm, h_tm, carry_scr,
              cw_ref, cb_ref, wg_ref, br_ref, bi_ref, lam_ref, after_block):
    tc = ts // LRU_TIME_CHUNKS
    rc = tc * nb
    for c in range(LRU_BLOCKS):
        cs = slice(c * LANES, (c + 1) * LANES)
        log_a_scale = (-LRU_C) * _softplus(-lam_ref[:, cs])
        hv = carry_scr[c]
        for kk in range(LRU_TIME_CHUNKS):
            k = LRU_TIME_CHUNKS - 1 - kk if reverse else kk
            r0 = k * rc
            xc = cb_ref[:, cs] + x_tm[c, r0:r0 + rc, :] * cw_ref[0:1, cs]
            for j in range(1, CONV_WIDTH):
                xc = xc + x_tm[c, r0 + j * nb:r0 + j * nb + rc, :] * cw_ref[j:j + 1, cs]
            z = jnp.dot(xc.astype(BF16), wg_ref[c], preferred_element_type=F32)
            r = jax.nn.sigmoid(z[:, :LANES] + br_ref[:, cs])
            ig = jax.nn.sigmoid(z[:, LANES:] + bi_ref[:, cs])
            log_a = r * log_a_scale
            a = jnp.exp(log_a)
            y = -jnp.tanh(log_a) * (a * a + 1.0)
            mult = jnp.where(y > 0.0, y * lax.rsqrt(y), 0.0)
            bb = mult * (ig * xc)
            for si in range(tc):
                s_idx = tc - 1 - si if reverse else si
                rr = slice(s_idx * nb, (s_idx + 1) * nb)
                hv = a[rr] * hv + bb[rr]
                h_tm[c, r0 + s_idx * nb:r0 + (s_idx + 1) * nb, :] = hv
        carry_scr[c] = hv
        after_block(c)


def _lru_scratch(ts, nb):
    rows = ts * nb
    return [pltpu.VMEM((LRU_BLOCKS, rows + (CONV_WIDTH - 1) * nb, LANES), F32),
            pltpu.VMEM((LRU_BLOCKS, rows, LANES), F32),
            pltpu.VMEM((LRU_BLOCKS, nb, LANES), F32)]


def _front_kernel(ts, x_ref, xh_ref, g_ref, w_ref, cw_ref, cb_ref, wg_ref, br_ref, bi_ref, lam_ref,
                  q_ref, kv_ref, xr_ref, gr_ref, ga_ref, gb_ref, hf_ref, x_tm, h_tm, carry_scr):
    i = pl.program_id(0)
    nt = pl.num_programs(0)
    nb, _, d = x_ref.shape
    rows = ts * nb
    lo = CONV_LEFT * nb
    g = g_ref[...]

    @pl.when(i == 0)
    def _():
        carry_scr[...] = jnp.zeros_like(carry_scr)
        x_tm[:, rows:rows + lo, :] = jnp.zeros((LRU_BLOCKS, lo, LANES), F32)

    xv = jnp.concatenate([x_ref[b] for b in range(nb)] + [xh_ref[:, 0, :], xh_ref[:, 1, :]], axis=0)
    h_all = _rms(xv, g).astype(BF16)
    h = h_all[:rows]

    c_xr = d + 2 * KV_DIM
    z_xr = jnp.dot(h_all, w_ref[:, c_xr:c_xr + d], preferred_element_type=F32)
    for c in range(LRU_BLOCKS):
        x_tm[c, 0:lo, :] = x_tm[c, rows:rows + lo, :]
    for c in range(LRU_BLOCKS):
        cs = slice(c * LANES, (c + 1) * LANES)
        for b in range(nb):
            x_tm[c, pl.ds(lo + b, ts, stride=nb), :] = z_xr[b * ts:(b + 1) * ts, cs]
        x_tm[c, lo + rows:lo + rows + nb, :] = jnp.where(i < nt - 1, z_xr[rows:rows + nb, cs], 0.0)
    for b in range(nb):
        xr_ref[b] = z_xr[b * ts:(b + 1) * ts].astype(BF16)

    outs = ((q_ref, 0, d, lambda z: z * (HEAD_DIM ** -0.5)),
            (kv_ref, d, 2 * KV_DIM, None),
            (gr_ref, c_xr + d, d, None),
            (ga_ref, c_xr + 2 * d, d, None),
            (gb_ref, c_xr + 3 * d, d, None))
    chunks = [(o_ref, c0, o0, fn) for o_ref, c0, width, fn in outs
              for o0 in range(0, width, PROJ_CHUNK)]

    def project(o_ref, c0, o0, fn):
        z = jnp.dot(h, w_ref[:, c0 + o0:c0 + o0 + PROJ_CHUNK], preferred_element_type=F32)
        if fn is not None:
            z = fn(z)
        z = z.astype(BF16)
        for b in range(nb):
            o_ref[b, :, o0:o0 + PROJ_CHUNK] = z[b * ts:(b + 1) * ts]

    def after_block(c):
        cs = slice(c * LANES, (c + 1) * LANES)
        for b in range(nb):
            hf_ref[b, :, cs] = h_tm[c, pl.ds(b, ts, stride=nb), :].astype(BF16)
        for ch in chunks[c::LRU_BLOCKS]:
            project(*ch)

    _lru_core(False, ts, nb, x_tm, h_tm, carry_scr,
              cw_ref, cb_ref, wg_ref, br_ref, bi_ref, lam_ref, after_block)


def _front(x3, g, w, conv_w, conv_b, wg, b_r, b_i, lam, ts):
    b, s, d = x3.shape
    assert b == SUBLANES and d == LRU_BLOCKS * LANES and w.shape[1] == 5 * d + 2 * KV_DIM
    nt = s // ts
    nh = s // SUBLANES
    r = ts // SUBLANES
    tile = lambda c: pl.BlockSpec((b, ts, c), lambda i: (0, i, 0))
    widths = (d, 2 * KV_DIM, d, d, d, d, d)
    return pl.pallas_call(
        functools.partial(_front_kernel, ts),
        grid=(nt,),
        in_specs=[tile(d),
                  pl.BlockSpec((b, SUBLANES, d), lambda i: (0, jnp.minimum((i + 1) * r, nh - 1), 0)),
                  _const_spec((1, d)), _const_spec(w.shape),
                  _const_spec((CONV_WIDTH, d)), _const_spec((1, d)), _const_spec(wg.shape),
                  _const_spec((1, d)), _const_spec((1, d)), _const_spec((1, d))],
        out_specs=[tile(c) for c in widths],
        out_shape=[jax.ShapeDtypeStruct((b, s, c), BF16) for c in widths],
        scratch_shapes=_lru_scratch(ts, b),
        compiler_params=_cparams(("arbitrary",)),
        name="front",
    )(x3, x3, g.reshape(1, d), w, conv_w, conv_b.reshape(1, d), wg, b_r.reshape(1, d), b_i.reshape(1, d),
      lam.reshape(1, d))


def _attn_kernel(tq, sink_ref, q_ref, kvp_ref, kvc_ref, kvn_ref, bias_ref, o_ref, kv_scr):
    i = pl.program_id(1)
    nt = pl.num_programs(1)
    nblk = tq // BLOCK
    kv_scr[0:BLOCK] = kvp_ref[...]
    kv_scr[BLOCK:BLOCK + tq] = kvc_ref[...]
    kv_scr[BLOCK + tq:] = kvn_ref[...]
    def scores(j, h0):
        heads = [h0 + h for h in range(ATTN_HEADS_PER_SLAB)]
        g = h0 // GROUP
        q4 = jnp.concatenate(
            [q_ref[j * BLOCK:(j + 1) * BLOCK, hh * HEAD_DIM:(hh + 1) * HEAD_DIM] for hh in heads], axis=0)
        k = kv_scr[j * BLOCK:(j + 3) * BLOCK, g * HEAD_DIM:(g + 1) * HEAD_DIM]
        return lax.dot_general(k, q4, (((1,), (1,)), ((), ())), preferred_element_type=F32)

    def finish(j, h0, st):
        heads = [h0 + h for h in range(ATTN_HEADS_PER_SLAB)]
        g = h0 // GROUP
        c0 = (h0 % GROUP) * BLOCK
        v = kv_scr[j * BLOCK:(j + 3) * BLOCK, KV_DIM + g * HEAD_DIM:KV_DIM + (g + 1) * HEAD_DIM]
        logits = st + bias_ref[g, :, c0:c0 + ATTN_HEADS_PER_SLAB * BLOCK]
        if j == 0:
            top = jnp.where(i == 0, -jnp.inf, logits[:BLOCK])
            logits = jnp.concatenate([top, logits[BLOCK:]], axis=0)
        if j == nblk - 1:
            bot = jnp.where(i == nt - 1, -jnp.inf, logits[2 * BLOCK:])
            logits = jnp.concatenate([logits[:2 * BLOCK], bot], axis=0)
        sink = jnp.concatenate([jnp.full((1, BLOCK), sink_ref[hh], F32) for hh in heads], axis=1)
        m = jnp.maximum(jnp.max(logits, axis=0, keepdims=True), sink)
        p = jnp.exp(logits - m)
        denom = jnp.sum(p, axis=0, keepdims=True) + jnp.exp(sink - m)
        pn = (p * (1.0 / denom)).astype(BF16)
        o = lax.dot_general(pn, v, (((0,), (0,)), ((), ())), preferred_element_type=F32)
        for h, hh in enumerate(heads):
            o_ref[j * BLOCK:(j + 1) * BLOCK, hh * HEAD_DIM:(hh + 1) * HEAD_DIM] = (
                o[h * BLOCK:(h + 1) * BLOCK].astype(BF16))

    slabs = [(j, h0) for j in range(nblk) for h0 in range(0, N_HEADS, ATTN_HEADS_PER_SLAB)]
    st_next = scores(*slabs[0])
    for n, (j, h0) in enumerate(slabs):
        st = st_next
        if n + 1 < len(slabs):
            st_next = scores(*slabs[n + 1])
        finish(j, h0, st)


def _attention(q, kv, sink, bias_tab, b, s, tq):
    t, d = q.shape
    nt = s // tq
    nb = s // BLOCK
    r = tq // BLOCK
    return pl.pallas_call(
        functools.partial(_attn_kernel, tq),
        grid=(b, nt),
        in_specs=[pl.BlockSpec(memory_space=pltpu.SMEM),
                  pl.BlockSpec((tq, d), lambda bi, i: (bi * nt + i, 0)),
                  pl.BlockSpec((BLOCK, 2 * KV_DIM), lambda bi, i: (bi * nb + jnp.maximum(i * r - 1, 0), 0)),
                  pl.BlockSpec((tq, 2 * KV_DIM), lambda bi, i: (bi * nt + i, 0)),
                  pl.BlockSpec((BLOCK, 2 * KV_DIM), lambda bi, i: (bi * nb + jnp.minimum((i + 1) * r, nb - 1), 0)),
                  _const_spec(bias_tab.shape)],
        out_specs=pl.BlockSpec((tq, d), lambda bi, i: (bi * nt + i, 0)),
        out_shape=jax.ShapeDtypeStruct((t, d), BF16),
        scratch_shapes=[pltpu.VMEM((tq + 2 * BLOCK, 2 * KV_DIM), BF16)],
        compiler_params=_cparams(("parallel", "parallel")),
        name="attention",
    )(sink.astype(F32), q, kv, kv, kv, bias_tab)


def _back_kernel(ts, x_ref, at_ref, ga_ref, gb_ref, xp_ref, xc_ref, xn_ref, hf_ref, gr_ref,
                 cw_ref, cb_ref, wg_ref, br_ref, bi_ref, lam_ref, woa_ref, wol_ref, wout_ref,
                 o_ref, x_tm, h_tm, carry_scr, y_scr, y_new):
    i = pl.program_id(0)
    nt = pl.num_programs(0) - 1
    t_idx = jnp.maximum(nt - 1 - i, 0)
    nb, _, d = x_ref.shape
    rows = ts * nb
    lo = CONV_LEFT * nb

    @pl.when(i == 0)
    def _():
        carry_scr[...] = jnp.zeros_like(carry_scr)
        y_scr[...] = jnp.zeros_like(y_scr)

    for b in range(nb):
        prev = jnp.where(t_idx > 0, xp_ref[b, BF16_ROWS - SUBLANES:, :].astype(F32), 0.0)
        nxt = jnp.where(t_idx < nt - 1, xn_ref[b, :SUBLANES, :].astype(F32), 0.0)
        for c in range(LRU_BLOCKS):
            cs = slice(c * LANES, (c + 1) * LANES)
            x_tm[c, pl.ds(lo + b, ts, stride=nb), :] = xc_ref[b, :, cs].astype(F32)
            x_tm[c, pl.ds(b, CONV_LEFT, stride=nb), :] = prev[SUBLANES - CONV_LEFT:, cs]
            x_tm[c, pl.ds(lo + rows + b, CONV_RIGHT, stride=nb), :] = nxt[:CONV_RIGHT, cs]

    attn = jnp.concatenate([at_ref[b] for b in range(nb)], axis=0)
    half = LRU_BLOCKS // 2
    cw_ = d // half
    ya_parts, yb_parts = [], []

    def after_block(c):
        cs = slice(c * LANES, (c + 1) * LANES)
        for b in range(nb):
            hb = h_tm[c, pl.ds(b, ts, stride=nb), :]
            yv = (hf_ref[b, :, cs].astype(F32) + hb) * _gelu_tanh(gr_ref[b, :, cs].astype(F32))
            y_new[b * ts:(b + 1) * ts, cs] = yv.astype(BF16)
        k = c % half
        cols = slice(k * cw_, (k + 1) * cw_)
        if c < half:
            ya_parts.append(jnp.dot(attn, woa_ref[:, cols], preferred_element_type=F32))
        else:
            yb_parts.append(jnp.dot(y_scr[...], wol_ref[:, cols], preferred_element_type=F32))

    _lru_core(True, ts, nb, x_tm, h_tm, carry_scr,
              cw_ref, cb_ref, wg_ref, br_ref, bi_ref, lam_ref, after_block)

    ya = jnp.concatenate(ya_parts, axis=1)
    yb = jnp.concatenate(yb_parts, axis=1)
    ga = jnp.concatenate([ga_ref[b] for b in range(nb)], axis=0).astype(F32)
    gb = jnp.concatenate([gb_ref[b] for b in range(nb)], axis=0).astype(F32)
    mix = jax.nn.sigmoid(ga) * ya + jax.nn.sigmoid(gb) * yb
    upd = jnp.dot(mix.astype(BF16), wout_ref[...], preferred_element_type=F32)
    for b in range(nb):
        o_ref[b] = x_ref[b] + upd[b * ts:(b + 1) * ts]

    y_scr[...] = y_new[...]


def _back(x3, attn3, ga3, gb3, xr3, hf3, gr3, conv_w, conv_b, wg, b_r, b_i, lam, woa, wol, wout, ts):
    b, s, d = x3.shape
    nt = s // ts
    nh = s // BF16_ROWS
    r = ts // BF16_ROWS
    tix = lambda i: jnp.maximum(nt - 1 - i, 0)
    mix_tix = lambda i: jnp.minimum(nt - i, nt - 1)
    tile = pl.BlockSpec((b, ts, d), lambda i: (0, tix(i), 0))
    mix_tile = pl.BlockSpec((b, ts, d), lambda i: (0, mix_tix(i), 0))
    return pl.pallas_call(
        functools.partial(_back_kernel, ts),
        grid=(nt + 1,),
        in_specs=[mix_tile, mix_tile, mix_tile, mix_tile,
                  pl.BlockSpec((b, BF16_ROWS, d), lambda i: (0, jnp.maximum(tix(i) * r - 1, 0), 0)),
                  tile,
                  pl.BlockSpec((b, BF16_ROWS, d), lambda i: (0, jnp.minimum((tix(i) + 1) * r, nh - 1), 0)),
                  tile, tile,
                  _const_spec((CONV_WIDTH, d)), _const_spec((1, d)), _const_spec(wg.shape),
                  _const_spec((1, d)), _const_spec((1, d)), _const_spec((1, d)),
                  _const_spec(woa.shape), _const_spec(wol.shape), _const_spec(wout.shape)],
        out_specs=mix_tile,
        out_shape=jax.ShapeDtypeStruct((b, s, d), F32),
        scratch_shapes=_lru_scratch(ts, b) + [pltpu.VMEM((ts * b, d), BF16)] * 2,
        compiler_params=_cparams(("arbitrary",)),
        name="back",
    )(x3, attn3, ga3, gb3, xr3, xr3, xr3, hf3, gr3, conv_w, conv_b.reshape(1, d), wg,
      b_r.reshape(1, d), b_i.reshape(1, d), lam.reshape(1, d), woa, wol, wout)


def _mlp_kernel(ff_chunk, final, x_ref, g2_ref, wup_ref, wdn_ref, gf_ref, o_ref):
    x1 = x_ref[...]
    h2 = _rms(x1, g2_ref[...]).astype(BF16)
    acc = x1
    dff = wup_ref.shape[1]
    for c0 in range(0, dff, ff_chunk):
        u = jnp.dot(h2, wup_ref[:, c0:c0 + ff_chunk], preferred_element_type=F32)
        u = jnp.square(jnp.maximum(u, 0.0)).astype(BF16)
        acc = acc + jnp.dot(u, wdn_ref[c0:c0 + ff_chunk, :], preferred_element_type=F32)
    if final:
        acc = _rms(acc, gf_ref[...])
    o_ref[...] = acc


def _mlp(xf, g2, wup, wdn, gf, final, tm):
    t, d = xf.shape
    tile = pl.BlockSpec((tm, d), lambda i: (i, 0))
    return pl.pallas_call(
        functools.partial(_mlp_kernel, min(1024, wup.shape[1]), final),
        grid=(t // tm,),
        in_specs=[tile, _const_spec((1, d)), _const_spec(wup.shape), _const_spec(wdn.shape),
                  _const_spec((1, d))],
        out_specs=tile,
        out_shape=jax.ShapeDtypeStruct((t, d), F32),
        compiler_params=_cparams(("parallel",)),
        name="mlp",
    )(xf, g2.reshape(1, d), wup, wdn, gf.reshape(1, d))


def _tile(n, pref):
    t = min(pref, n)
    assert n % t == 0, (n, t)
    return t


def kernel(x, w_in, w_o_attn, w_o_lru, w_out, attn_sink, rel_bias, conv_w, conv_b, w_rgate, b_rgate,
           w_igate, b_igate, lru_lambda, norm1_g, norm2_g, w_mlp_up, w_mlp_down, final_norm_g):
    b, s, d = x.shape
    depth = w_in.shape[0]
    assert d == N_HEADS * HEAD_DIM and s % BLOCK == 0
    t = b * s
    tm = _tile(t, 512)
    tq = _tile(s, 512)
    ts = _tile(s, 64)

    x3 = x
    bias_tab = _bias_table(rel_bias)
    for l in range(depth):
        wg = [jnp.concatenate([w_rgate[l, dr], w_igate[l, dr]], axis=-1).astype(BF16) for dr in range(2)]
        q, kv, xr, gr, ga, gb, hf = _front(x3, norm1_g[l], w_in[l].astype(BF16), conv_w[l], conv_b[l], wg[0],
                                           b_rgate[l, 0], b_igate[l, 0], lru_lambda[l, 0], ts)
        attn = _attention(q.reshape(t, d), kv.reshape(t, 2 * KV_DIM), attn_sink[l], bias_tab, b, s, tq)
        x1 = _back(x3, attn.reshape(b, s, d), ga, gb, xr, hf, gr, conv_w[l], conv_b[l], wg[1],
                   b_rgate[l, 1], b_igate[l, 1], lru_lambda[l, 1], w_o_attn[l].astype(BF16),
                   w_o_lru[l].astype(BF16), w_out[l].astype(BF16), ts)
        x3 = _mlp(x1.reshape(t, d), norm2_g[l], w_mlp_up[l].astype(BF16), w_mlp_down[l].astype(BF16),
                  final_norm_g, l == depth - 1, tm).reshape(b, s, d)
    return x3
```

```python
import functools
import math

import jax
import jax.numpy as jnp
from jax import lax
from jax.experimental import pallas as pl
from jax.experimental.pallas import tpu as pltpu

N_HEADS = 8
N_KV_HEADS = 2
HEAD_DIM = 128
GROUP = N_HEADS // N_KV_HEADS
KV_DIM = N_KV_HEADS * HEAD_DIM
WINDOW = 128
BLOCK = 128
NUM_BUCKETS = 32
MAX_DISTANCE = 128
LRU_BLOCKS = 8
LRU_C = 8.0
CONV_WIDTH = 4
CONV_LEFT = 2
CONV_RIGHT = CONV_WIDTH - 1 - CONV_LEFT
EPS = 1e-6

SUBLANES = 8
LANES = 128
BF16_ROWS = 16
PROJ_CHUNK = 512
LRU_TIME_CHUNKS = 1
ATTN_HEADS_PER_SLAB = 1
VMEM_LIMIT = 56 * 1024 * 1024

F32 = jnp.float32
BF16 = jnp.bfloat16


def _cparams(sem, flags=None):
    return pltpu.CompilerParams(dimension_semantics=sem, vmem_limit_bytes=VMEM_LIMIT, flags=flags)


def _const_spec(shape):
    nd = len(shape)
    return pl.BlockSpec(shape, lambda *_: (0,) * nd, pipeline_mode=pl.Buffered(1))


def _rms(xv, g):
    ms = jnp.mean(xv * xv, axis=-1, keepdims=True)
    return xv * lax.rsqrt(ms + EPS) * g


def _t5_bucket(rel):
    half = NUM_BUCKETS // 2
    max_exact = half // 2
    n = jnp.abs(rel)
    large = max_exact + (jnp.log(jnp.maximum(n, 1).astype(F32) / max_exact)
                         / math.log(MAX_DISTANCE / max_exact) * (half - max_exact)).astype(jnp.int32)
    large = jnp.minimum(large, half - 1)
    return jnp.where(rel > 0, half, 0) + jnp.where(n < max_exact, n, large)


def _bias_kernel(rb_ref, bucket_ref, o_ref):
    h = pl.program_id(0)
    bk = bucket_ref[...]
    acc = jnp.zeros(bk.shape, F32)
    for b in range(NUM_BUCKETS):
        acc = jnp.where(bk == b, rb_ref[b * N_HEADS + h], acc)
    j = lax.broadcasted_iota(jnp.int32, bk.shape, 0)
    q = lax.broadcasted_iota(jnp.int32, bk.shape, 1)
    rel = j - BLOCK - q
    o_ref[0] = jnp.where(jnp.abs(rel) <= WINDOW, acc, -jnp.inf)


def _bias_table(rel_bias):
    q_off = jnp.arange(BLOCK, dtype=jnp.int32)[None, :]
    j_off = jnp.arange(3 * BLOCK, dtype=jnp.int32)[:, None] - BLOCK
    bucket = _t5_bucket(j_off - q_off)
    return pl.pallas_call(
        _bias_kernel,
        grid=(N_HEADS,),
        in_specs=[pl.BlockSpec(memory_space=pltpu.SMEM),
                  pl.BlockSpec((3 * BLOCK, BLOCK), lambda h: (0, 0))],
        out_specs=pl.BlockSpec((1, 3 * BLOCK, BLOCK), lambda h: (h // GROUP, 0, h % GROUP)),
        out_shape=jax.ShapeDtypeStruct((N_KV_HEADS, 3 * BLOCK, GROUP * BLOCK), F32),
        name="bias_table",
    )(rel_bias.astype(F32).reshape(-1), bucket)


def _gelu_tanh(x):
    k = math.sqrt(2.0 / math.pi)
    hx = 0.5 * x
    return hx + hx * jnp.tanh(x * (k + (k * 0.044715) * (x * x)))


def _softplus(x):
    return jnp.maximum(x, 0.0) + jnp.log1p(jnp.exp(-jnp.abs(x)))


def _lru_core(reverse, ts, nb, x_tm, h_tm, carry_scr,
              cw_ref, cb_ref, wg_ref, br_ref, bi_ref, lam_ref, after_block):
    tc = ts // LRU_TIME_CHUNKS
    rc = tc * nb
    for c in range(LRU_BLOCKS):
        cs = slice(c * LANES, (c + 1) * LANES)
        log_a_scale = (-LRU_C) * _softplus(-lam_ref[:, cs])
        hv = carry_scr[c]
        for kk in range(LRU_TIME_CHUNKS):
            k = LRU_TIME_CHUNKS - 1 - kk if reverse else kk
            r0 = k * rc
            xc = cb_ref[:, cs] + x_tm[c, r0:r0 + rc, :] * cw_ref[0:1, cs]
            for j in range(1, CONV_WIDTH):
                xc = xc + x_tm[c, r0 + j * nb:r0 + j * nb + rc, :] * cw_ref[j:j + 1, cs]
            z = jnp.dot(xc.astype(BF16), wg_ref[c], preferred_element_type=F32)
            r = jax.nn.sigmoid(z[:, :LANES] + br_ref[:, cs])
            ig = jax.nn.sigmoid(z[:, LANES:] + bi_ref[:, cs])
            log_a = r * log_a_scale
            a = jnp.exp(log_a)
            y = -jnp.tanh(log_a) * (a * a + 1.0)
            mult = jnp.where(y > 0.0, y * lax.rsqrt(y), 0.0)
            bb = mult * (ig * xc)
            for si in range(tc):
                s_idx = tc - 1 - si if reverse else si
                rr = slice(s_idx * nb, (s_idx + 1) * nb)
                hv = a[rr] * hv + bb[rr]
                h_tm[c, r0 + s_idx * nb:r0 + (s_idx + 1) * nb, :] = hv
        carry_scr[c] = hv
        after_block(c)


def _act_cols(d):
    cols = {name: k * d for k, name in enumerate(("gr", "xr", "hf", "q", "ga", "gb"))}
    cols["kv"] = 6 * d
    return cols, 6 * d + 2 * KV_DIM


def _lru_scratch(ts, nb):
    rows = ts * nb
    return [pltpu.VMEM((LRU_BLOCKS, rows + (CONV_WIDTH - 1) * nb, LANES), F32),
            pltpu.VMEM((LRU_BLOCKS, rows, LANES), F32),
            pltpu.VMEM((LRU_BLOCKS, nb, LANES), F32)]


def _front_kernel(ts, x_ref, xh_ref, g_ref, w_ref, cw_ref, cb_ref, wg_ref, br_ref, bi_ref, lam_ref,
                  act_ref, x_tm, h_tm, carry_scr):
    i = pl.program_id(0)
    nt = pl.num_programs(0)
    nb, _, d = x_ref.shape
    rows = ts * nb
    lo = CONV_LEFT * nb
    g = g_ref[...]
    col, _ = _act_cols(d)

    @pl.when(i == 0)
    def _():
        carry_scr[...] = jnp.zeros_like(carry_scr)
        x_tm[:, rows:rows + lo, :] = jnp.zeros((LRU_BLOCKS, lo, LANES), F32)

    xv = jnp.concatenate([x_ref[b] for b in range(nb)] + [xh_ref[:, 0, :], xh_ref[:, 1, :]], axis=0)
    h_all = _rms(xv, g).astype(BF16)
    h = h_all[:rows]

    c_xr = d + 2 * KV_DIM
    z_xr = jnp.dot(h_all, w_ref[:, c_xr:c_xr + d], preferred_element_type=F32)
    for c in range(LRU_BLOCKS):
        x_tm[c, 0:lo, :] = x_tm[c, rows:rows + lo, :]
    for c in range(LRU_BLOCKS):
        cs = slice(c * LANES, (c + 1) * LANES)
        for b in range(nb):
            x_tm[c, pl.ds(lo + b, ts, stride=nb), :] = z_xr[b * ts:(b + 1) * ts, cs]
        x_tm[c, lo + rows:lo + rows + nb, :] = jnp.where(i < nt - 1, z_xr[rows:rows + nb, cs], 0.0)
    for b in range(nb):
        act_ref[b, :, col["xr"]:col["xr"] + d] = z_xr[b * ts:(b + 1) * ts].astype(BF16)

    outs = ((0, col["q"], d, lambda z: z * (HEAD_DIM ** -0.5)),
            (d, col["kv"], 2 * KV_DIM, None),
            (c_xr + d, col["gr"], d, None),
            (c_xr + 2 * d, col["ga"], d, None),
            (c_xr + 3 * d, col["gb"], d, None))
    chunks = [(w0 + o0, a0 + o0, fn) for w0, a0, width, fn in outs
              for o0 in range(0, width, PROJ_CHUNK)]

    def project(w0, a0, fn):
        z = jnp.dot(h, w_ref[:, w0:w0 + PROJ_CHUNK], preferred_element_type=F32)
        if fn is not None:
            z = fn(z)
        z = z.astype(BF16)
        for b in range(nb):
            act_ref[b, :, a0:a0 + PROJ_CHUNK] = z[b * ts:(b + 1) * ts]

    def after_block(c):
        a0 = col["hf"] + c * LANES
        for b in range(nb):
            act_ref[b, :, a0:a0 + LANES] = h_tm[c, pl.ds(b, ts, stride=nb), :].astype(BF16)
        for ch in chunks[c::LRU_BLOCKS]:
            project(*ch)

    _lru_core(False, ts, nb, x_tm, h_tm, carry_scr,
              cw_ref, cb_ref, wg_ref, br_ref, bi_ref, lam_ref, after_block)


def _front(x3, g, w, conv_w, conv_b, wg, b_r, b_i, lam, ts):
    b, s, d = x3.shape
    assert b == SUBLANES and d == LRU_BLOCKS * LANES and w.shape[1] == 5 * d + 2 * KV_DIM
    nt = s // ts
    nh = s // SUBLANES
    r = ts // SUBLANES
    tile = lambda c: pl.BlockSpec((b, ts, c), lambda i: (0, i, 0))
    _, act_width = _act_cols(d)
    return pl.pallas_call(
        functools.partial(_front_kernel, ts),
        grid=(nt,),
        in_specs=[tile(d),
                  pl.BlockSpec((b, SUBLANES, d), lambda i: (0, jnp.minimum((i + 1) * r, nh - 1), 0)),
                  _const_spec((1, d)), _const_spec(w.shape),
                  _const_spec((CONV_WIDTH, d)), _const_spec((1, d)), _const_spec(wg.shape),
                  _const_spec((1, d)), _const_spec((1, d)), _const_spec((1, d))],
        out_specs=tile(act_width),
        out_shape=jax.ShapeDtypeStruct((b, s, act_width), BF16),
        scratch_shapes=_lru_scratch(ts, b),
        compiler_params=_cparams(("arbitrary",)),
        name="front",
    )(x3, x3, g.reshape(1, d), w, conv_w, conv_b.reshape(1, d), wg, b_r.reshape(1, d), b_i.reshape(1, d),
      lam.reshape(1, d))


def _attn_kernel(tq, sink_ref, q_ref, kvp_ref, kvc_ref, kvn_ref, bias_ref, o_ref, kv_scr):
    i = pl.program_id(1)
    nt = pl.num_programs(1)
    nblk = tq // BLOCK
    kv_scr[0:BLOCK] = kvp_ref[...]
    kv_scr[BLOCK:BLOCK + tq] = kvc_ref[...]
    kv_scr[BLOCK + tq:] = kvn_ref[...]
    def scores(j, h0):
        heads = [h0 + h for h in range(ATTN_HEADS_PER_SLAB)]
        g = h0 // GROUP
        q4 = jnp.concatenate(
            [q_ref[j * BLOCK:(j + 1) * BLOCK, hh * HEAD_DIM:(hh + 1) * HEAD_DIM] for hh in heads], axis=0)
        k = kv_scr[j * BLOCK:(j + 3) * BLOCK, g * HEAD_DIM:(g + 1) * HEAD_DIM]
        return lax.dot_general(k, q4, (((1,), (1,)), ((), ())), preferred_element_type=F32)

    def finish(j, h0, st):
        heads = [h0 + h for h in range(ATTN_HEADS_PER_SLAB)]
        g = h0 // GROUP
        c0 = (h0 % GROUP) * BLOCK
        v = kv_scr[j * BLOCK:(j + 3) * BLOCK, KV_DIM + g * HEAD_DIM:KV_DIM + (g + 1) * HEAD_DIM]
        logits = st + bias_ref[g, :, c0:c0 + ATTN_HEADS_PER_SLAB * BLOCK]
        if j == 0:
            top = jnp.where(i == 0, -jnp.inf, logits[:BLOCK])
            logits = jnp.concatenate([top, logits[BLOCK:]], axis=0)
        if j == nblk - 1:
            bot = jnp.where(i == nt - 1, -jnp.inf, logits[2 * BLOCK:])
            logits = jnp.concatenate([logits[:2 * BLOCK], bot], axis=0)
        sink = jnp.concatenate([jnp.full((1, BLOCK), sink_ref[hh], F32) for hh in heads], axis=1)
        m = jnp.maximum(jnp.max(logits, axis=0, keepdims=True), sink)
        p = jnp.exp(logits - m)
        denom = jnp.sum(p, axis=0, keepdims=True) + jnp.exp(sink - m)
        pn = (p * (1.0 / denom)).astype(BF16)
        o = lax.dot_general(pn, v, (((0,), (0,)), ((), ())), preferred_element_type=F32)
        for h, hh in enumerate(heads):
            o_ref[j * BLOCK:(j + 1) * BLOCK, hh * HEAD_DIM:(hh + 1) * HEAD_DIM] = (
                o[h * BLOCK:(h + 1) * BLOCK].astype(BF16))

    slabs = [(j, h0) for j in range(nblk) for h0 in range(0, N_HEADS, ATTN_HEADS_PER_SLAB)]
    st_next = scores(*slabs[0])
    for n, (j, h0) in enumerate(slabs):
        st = st_next
        if n + 1 < len(slabs):
            st_next = scores(*slabs[n + 1])
        finish(j, h0, st)


def _attention(act, d, sink, bias_tab, b, s, tq):
    t = act.shape[0]
    col, _ = _act_cols(d)
    qc = col["q"] // d
    kvc = col["kv"] // (2 * KV_DIM)
    nt = s // tq
    nb = s // BLOCK
    r = tq // BLOCK
    return pl.pallas_call(
        functools.partial(_attn_kernel, tq),
        grid=(b, nt),
        in_specs=[pl.BlockSpec(memory_space=pltpu.SMEM),
                  pl.BlockSpec((tq, d), lambda bi, i: (bi * nt + i, qc)),
                  pl.BlockSpec((BLOCK, 2 * KV_DIM), lambda bi, i: (bi * nb + jnp.maximum(i * r - 1, 0), kvc)),
                  pl.BlockSpec((tq, 2 * KV_DIM), lambda bi, i: (bi * nt + i, kvc)),
                  pl.BlockSpec((BLOCK, 2 * KV_DIM),
                               lambda bi, i: (bi * nb + jnp.minimum((i + 1) * r, nb - 1), kvc)),
                  _const_spec(bias_tab.shape)],
        out_specs=pl.BlockSpec((tq, d), lambda bi, i: (bi * nt + i, 0)),
        out_shape=jax.ShapeDtypeStruct((t, d), BF16),
        scratch_shapes=[pltpu.VMEM((tq + 2 * BLOCK, 2 * KV_DIM), BF16)],
        compiler_params=_cparams(("parallel", "parallel")),
        name="attention",
    )(sink.astype(F32), act, act, act, act, bias_tab)


def _back_kernel(ts, x_ref, at_ref, gab_ref, xp_ref, lru_ref, xn_ref,
                 cw_ref, cb_ref, wg_ref, br_ref, bi_ref, lam_ref, woa_ref, wol_ref, wout_ref,
                 o_ref, x_tm, h_tm, carry_scr, y_scr, y_new):
    i = pl.program_id(0)
    nt = pl.num_programs(0) - 1
    t_idx = jnp.maximum(nt - 1 - i, 0)
    nb, _, d = x_ref.shape
    rows = ts * nb
    lo = CONV_LEFT * nb

    @pl.when(i == 0)
    def _():
        carry_scr[...] = jnp.zeros_like(carry_scr)
        y_scr[...] = jnp.zeros_like(y_scr)

    for b in range(nb):
        prev = jnp.where(t_idx > 0, xp_ref[b, BF16_ROWS - SUBLANES:, :].astype(F32), 0.0)
        nxt = jnp.where(t_idx < nt - 1, xn_ref[b, :SUBLANES, :].astype(F32), 0.0)
        for c in range(LRU_BLOCKS):
            cs = slice(c * LANES, (c + 1) * LANES)
            x_tm[c, pl.ds(lo + b, ts, stride=nb), :] = lru_ref[b, :, d + c * LANES:d + (c + 1) * LANES].astype(F32)
            x_tm[c, pl.ds(b, CONV_LEFT, stride=nb), :] = prev[SUBLANES - CONV_LEFT:, cs]
            x_tm[c, pl.ds(lo + rows + b, CONV_RIGHT, stride=nb), :] = nxt[:CONV_RIGHT, cs]

    attn = jnp.concatenate([at_ref[b] for b in range(nb)], axis=0)
    half = LRU_BLOCKS // 2
    cw_ = d // half
    ya_parts, yb_parts = [], []

    def after_block(c):
        cs = slice(c * LANES, (c + 1) * LANES)
        for b in range(nb):
            hb = h_tm[c, pl.ds(b, ts, stride=nb), :]
            hf = lru_ref[b, :, 2 * d + c * LANES:2 * d + (c + 1) * LANES].astype(F32)
            yv = (hf + hb) * _gelu_tanh(lru_ref[b, :, cs].astype(F32))
            y_new[b * ts:(b + 1) * ts, cs] = yv.astype(BF16)
        k = c % half
        cols = slice(k * cw_, (k + 1) * cw_)
        if c < half:
            ya_parts.append(jnp.dot(attn, woa_ref[:, cols], preferred_element_type=F32))
        else:
            yb_parts.append(jnp.dot(y_scr[...], wol_ref[:, cols], preferred_element_type=F32))

    _lru_core(True, ts, nb, x_tm, h_tm, carry_scr,
              cw_ref, cb_ref, wg_ref, br_ref, bi_ref, lam_ref, after_block)

    ya = jnp.concatenate(ya_parts, axis=1)
    yb = jnp.concatenate(yb_parts, axis=1)
    ga = jnp.concatenate([gab_ref[b, :, :d] for b in range(nb)], axis=0).astype(F32)
    gb = jnp.concatenate([gab_ref[b, :, d:] for b in range(nb)], axis=0).astype(F32)
    mix = jax.nn.sigmoid(ga) * ya + jax.nn.sigmoid(gb) * yb
    upd = jnp.dot(mix.astype(BF16), wout_ref[...], preferred_element_type=F32)
    for b in range(nb):
        o_ref[b] = x_ref[b] + upd[b * ts:(b + 1) * ts]

    y_scr[...] = y_new[...]


def _back(x3, attn3, act3, conv_w, conv_b, wg, b_r, b_i, lam, woa, wol, wout, ts):
    b, s, d = x3.shape
    col, _ = _act_cols(d)
    assert col["gr"] == 0 and col["xr"] == d and col["hf"] == 2 * d and col["gb"] == col["ga"] + d
    nt = s // ts
    nh = s // BF16_ROWS
    r = ts // BF16_ROWS
    tix = lambda i: jnp.maximum(nt - 1 - i, 0)
    mix_tix = lambda i: jnp.minimum(nt - i, nt - 1)
    mix_tile = pl.BlockSpec((b, ts, d), lambda i: (0, mix_tix(i), 0))
    xr_halo = lambda tb: pl.BlockSpec((b, BF16_ROWS, d), lambda i: (0, tb(i), col["xr"] // d))
    return pl.pallas_call(
        functools.partial(_back_kernel, ts),
        grid=(nt + 1,),
        in_specs=[mix_tile, mix_tile,
                  pl.BlockSpec((b, ts, 2 * d), lambda i: (0, mix_tix(i), col["ga"] // (2 * d))),
                  xr_halo(lambda i: jnp.maximum(tix(i) * r - 1, 0)),
                  pl.BlockSpec((b, ts, 3 * d), lambda i: (0, tix(i), 0)),
                  xr_halo(lambda i: jnp.minimum((tix(i) + 1) * r, nh - 1)),
                  _const_spec((CONV_WIDTH, d)), _const_spec((1, d)), _const_spec(wg.shape),
                  _const_spec((1, d)), _const_spec((1, d)), _const_spec((1, d)),
                  _const_spec(woa.shape), _const_spec(wol.shape), _const_spec(wout.shape)],
        out_specs=mix_tile,
        out_shape=jax.ShapeDtypeStruct((b, s, d), F32),
        scratch_shapes=_lru_scratch(ts, b) + [pltpu.VMEM((ts * b, d), BF16)] * 2,
        compiler_params=_cparams(("arbitrary",)),
        name="back",
    )(x3, attn3, act3, act3, act3, act3, conv_w, conv_b.reshape(1, d), wg,
      b_r.reshape(1, d), b_i.reshape(1, d), lam.reshape(1, d), woa, wol, wout)


def _mlp_kernel(ff_chunk, final, x_ref, g2_ref, wup_ref, wdn_ref, gf_ref, o_ref):
    x1 = x_ref[...]
    h2 = _rms(x1, g2_ref[...]).astype(BF16)
    acc = x1
    dff = wup_ref.shape[1]
    for c0 in range(0, dff, ff_chunk):
        u = jnp.dot(h2, wup_ref[:, c0:c0 + ff_chunk], preferred_element_type=F32)
        u = jnp.square(jnp.maximum(u, 0.0)).astype(BF16)
        acc = acc + jnp.dot(u, wdn_ref[c0:c0 + ff_chunk, :], preferred_element_type=F32)
    if final:
        acc = _rms(acc, gf_ref[...])
    o_ref[...] = acc


def _mlp(xf, g2, wup, wdn, gf, final, tm):
    t, d = xf.shape
    tile = pl.BlockSpec((tm, d), lambda i: (i, 0))
    return pl.pallas_call(
        functools.partial(_mlp_kernel, min(1024, wup.shape[1]), final),
        grid=(t // tm,),
        in_specs=[tile, _const_spec((1, d)), _const_spec(wup.shape), _const_spec(wdn.shape),
                  _const_spec((1, d))],
        out_specs=tile,
        out_shape=jax.ShapeDtypeStruct((t, d), F32),
        compiler_params=_cparams(("parallel",)),
        name="mlp",
    )(xf, g2.reshape(1, d), wup, wdn, gf.reshape(1, d))


def _tile(n, pref):
    t = min(pref, n)
    assert n % t == 0, (n, t)
    return t


def kernel(x, w_in, w_o_attn, w_o_lru, w_out, attn_sink, rel_bias, conv_w, conv_b, w_rgate, b_rgate,
           w_igate, b_igate, lru_lambda, norm1_g, norm2_g, w_mlp_up, w_mlp_down, final_norm_g):
    b, s, d = x.shape
    depth = w_in.shape[0]
    assert d == N_HEADS * HEAD_DIM and s % BLOCK == 0
    t = b * s
    tm = _tile(t, 1024)
    tq = _tile(s, 1024)
    ts = _tile(s, 64)

    x3 = x
    bias_tab = _bias_table(rel_bias)
    for l in range(depth):
        wg = [jnp.concatenate([w_rgate[l, dr], w_igate[l, dr]], axis=-1).astype(BF16) for dr in range(2)]
        act = _front(x3, norm1_g[l], w_in[l].astype(BF16), conv_w[l], conv_b[l], wg[0],
                     b_rgate[l, 0], b_igate[l, 0], lru_lambda[l, 0], ts)
        attn = _attention(act.reshape(t, act.shape[-1]), d, attn_sink[l], bias_tab, b, s, tq)
        x1 = _back(x3, attn.reshape(b, s, d), act, conv_w[l], conv_b[l], wg[1],
                   b_rgate[l, 1], b_igate[l, 1], lru_lambda[l, 1], w_o_attn[l].astype(BF16),
                   w_o_lru[l].astype(BF16), w_out[l].astype(BF16), ts)
        x3 = _mlp(x1.reshape(t, d), norm2_g[l], w_mlp_up[l].astype(BF16), w_mlp_down[l].astype(BF16),
                  final_norm_g, l == depth - 1, tm).reshape(b, s, d)
    return x3
```

```python
import functools
import math

import jax
import jax.numpy as jnp
from jax import lax
from jax.experimental import pallas as pl
from jax.experimental.pallas import tpu as pltpu

N_HEADS = 8
N_KV_HEADS = 2
HEAD_DIM = 128
GROUP = N_HEADS // N_KV_HEADS
KV_DIM = N_KV_HEADS * HEAD_DIM
WINDOW = 128
BLOCK = 128
NUM_BUCKETS = 32
MAX_DISTANCE = 128
LRU_BLOCKS = 8
LRU_C = 8.0
CONV_WIDTH = 4
CONV_LEFT = 2
CONV_RIGHT = CONV_WIDTH - 1 - CONV_LEFT
EPS = 1e-6
LOG2E = math.log2(math.e)

SUBLANES = 8
LANES = 128
BF16_ROWS = 16
PROJ_CHUNK = 512
VMEM_LIMIT = 56 * 1024 * 1024

F32 = jnp.float32
BF16 = jnp.bfloat16


def _cparams(sem, flags=None):
    return pltpu.CompilerParams(dimension_semantics=sem, vmem_limit_bytes=VMEM_LIMIT, flags=flags)


def _const_spec(shape):
    nd = len(shape)
    return pl.BlockSpec(shape, lambda *_: (0,) * nd, pipeline_mode=pl.Buffered(1))


def _rms(xv, g):
    ms = jnp.mean(xv * xv, axis=-1, keepdims=True)
    return xv * lax.rsqrt(ms + EPS) * g


def _t5_bucket(rel):
    half = NUM_BUCKETS // 2
    max_exact = half // 2
    n = jnp.abs(rel)
    large = max_exact + (jnp.log(jnp.maximum(n, 1).astype(F32) / max_exact)
                         / math.log(MAX_DISTANCE / max_exact) * (half - max_exact)).astype(jnp.int32)
    large = jnp.minimum(large, half - 1)
    return jnp.where(rel > 0, half, 0) + jnp.where(n < max_exact, n, large)


def _bias_kernel(rb_ref, bucket_ref, o_ref):
    h = pl.program_id(0)
    bk = bucket_ref[...]
    acc = jnp.zeros(bk.shape, F32)
    for b in range(NUM_BUCKETS):
        acc = jnp.where(bk == b, rb_ref[b * N_HEADS + h], acc)
    j = lax.broadcasted_iota(jnp.int32, bk.shape, 0)
    q = lax.broadcasted_iota(jnp.int32, bk.shape, 1)
    rel = j - BLOCK - q
    o_ref[0] = jnp.where(jnp.abs(rel) <= WINDOW, acc * LOG2E, -jnp.inf)


def _bias_table(rel_bias):
    q_off = jnp.arange(BLOCK, dtype=jnp.int32)[None, :]
    j_off = jnp.arange(3 * BLOCK, dtype=jnp.int32)[:, None] - BLOCK
    bucket = _t5_bucket(j_off - q_off)
    return pl.pallas_call(
        _bias_kernel,
        grid=(N_HEADS,),
        in_specs=[pl.BlockSpec(memory_space=pltpu.SMEM),
                  pl.BlockSpec((3 * BLOCK, BLOCK), lambda h: (0, 0))],
        out_specs=pl.BlockSpec((1, 3 * BLOCK, BLOCK), lambda h: (h // GROUP, 0, h % GROUP)),
        out_shape=jax.ShapeDtypeStruct((N_KV_HEADS, 3 * BLOCK, GROUP * BLOCK), F32),
        name="bias_table",
    )(rel_bias.astype(F32).reshape(-1), bucket)


def _gelu_tanh(x):
    k = math.sqrt(2.0 / math.pi)
    hx = 0.5 * x
    return hx + hx * jnp.tanh(x * (k + (k * 0.044715) * (x * x)))


def _softplus(x):
    return jnp.maximum(x, 0.0) + jnp.log1p(jnp.exp(-jnp.abs(x)))


def _lru_core(reverse, ts, nb, x_tm, h_tm, carry_scr,
              cw_ref, cb_ref, wg_ref, br_ref, bi_ref, lam_ref, after_block):
    rows = ts * nb
    for c in range(LRU_BLOCKS):
        cs = slice(c * LANES, (c + 1) * LANES)
        xc = cb_ref[:, cs] + x_tm[c, 0:rows, :] * cw_ref[0:1, cs]
        for j in range(1, CONV_WIDTH):
            xc = xc + x_tm[c, j * nb:j * nb + rows, :] * cw_ref[j:j + 1, cs]
        z = jnp.dot(xc.astype(BF16), wg_ref[c], preferred_element_type=F32)
        r = jax.nn.sigmoid(z[:, :LANES] + br_ref[:, cs])
        ig = jax.nn.sigmoid(z[:, LANES:] + bi_ref[:, cs])
        log_a = r * ((-LRU_C) * _softplus(-lam_ref[:, cs]))
        a = jnp.exp(log_a)
        y = -jnp.tanh(log_a) * (a * a + 1.0)
        mult = jnp.where(y > 0.0, y * lax.rsqrt(y), 0.0)
        bb = mult * (ig * xc)
        hv = carry_scr[c]
        for si in range(ts):
            s_idx = ts - 1 - si if reverse else si
            rr = slice(s_idx * nb, (s_idx + 1) * nb)
            hv = a[rr] * hv + bb[rr]
            h_tm[c, rr, :] = hv
        carry_scr[c] = hv
        after_block(c)


def _act_cols(d):
    cols = {name: k * d for k, name in enumerate(("gr", "xr", "hf", "q", "ga", "gb"))}
    cols["kv"] = 6 * d
    return cols, 6 * d + 2 * KV_DIM


def _lru_scratch(ts, nb):
    rows = ts * nb
    return [pltpu.VMEM((LRU_BLOCKS, rows + (CONV_WIDTH - 1) * nb, LANES), F32),
            pltpu.VMEM((LRU_BLOCKS, rows, LANES), F32),
            pltpu.VMEM((LRU_BLOCKS, nb, LANES), F32)]


def _front_kernel(ts, x_ref, xh_ref, g_ref, w_ref, cw_ref, cb_ref, wg_ref, br_ref, bi_ref, lam_ref,
                  act_ref, x_tm, h_tm, carry_scr):
    i = pl.program_id(0)
    nt = pl.num_programs(0)
    nb, _, d = x_ref.shape
    rows = ts * nb
    lo = CONV_LEFT * nb
    g = g_ref[...]
    col, _ = _act_cols(d)

    @pl.when(i == 0)
    def _():
        carry_scr[...] = jnp.zeros_like(carry_scr)
        x_tm[:, rows:rows + lo, :] = jnp.zeros((LRU_BLOCKS, lo, LANES), F32)

    xv = jnp.concatenate([x_ref[b] for b in range(nb)] + [xh_ref[:, 0, :], xh_ref[:, 1, :]], axis=0)
    h_all = _rms(xv, g).astype(BF16)
    h = h_all[:rows]

    c_xr = d + 2 * KV_DIM
    z_xr = jnp.dot(h_all, w_ref[:, c_xr:c_xr + d], preferred_element_type=F32)
    for c in range(LRU_BLOCKS):
        x_tm[c, 0:lo, :] = x_tm[c, rows:rows + lo, :]
    for c in range(LRU_BLOCKS):
        cs = slice(c * LANES, (c + 1) * LANES)
        for b in range(nb):
            x_tm[c, pl.ds(lo + b, ts, stride=nb), :] = z_xr[b * ts:(b + 1) * ts, cs]
        x_tm[c, lo + rows:lo + rows + nb, :] = jnp.where(i < nt - 1, z_xr[rows:rows + nb, cs], 0.0)
    for b in range(nb):
        act_ref[b, :, col["xr"]:col["xr"] + d] = z_xr[b * ts:(b + 1) * ts].astype(BF16)

    outs = ((0, col["q"], d, lambda z: z * (HEAD_DIM ** -0.5 * LOG2E)),
            (d, col["kv"], 2 * KV_DIM, None),
            (c_xr + d, col["gr"], d, None),
            (c_xr + 2 * d, col["ga"], d, None),
            (c_xr + 3 * d, col["gb"], d, None))
    chunks = [(w0 + o0, a0 + o0, fn) for w0, a0, width, fn in outs
              for o0 in range(0, width, PROJ_CHUNK)]

    def project(w0, a0, fn):
        z = jnp.dot(h, w_ref[:, w0:w0 + PROJ_CHUNK], preferred_element_type=F32)
        if fn is not None:
            z = fn(z)
        z = z.astype(BF16)
        for b in range(nb):
            act_ref[b, :, a0:a0 + PROJ_CHUNK] = z[b * ts:(b + 1) * ts]

    def after_block(c):
        a0 = col["hf"] + c * LANES
        for b in range(nb):
            act_ref[b, :, a0:a0 + LANES] = h_tm[c, pl.ds(b, ts, stride=nb), :].astype(BF16)
        for ch in chunks[c::LRU_BLOCKS]:
            project(*ch)

    _lru_core(False, ts, nb, x_tm, h_tm, carry_scr,
              cw_ref, cb_ref, wg_ref, br_ref, bi_ref, lam_ref, after_block)


def _front(x3, g, w, conv_w, conv_b, wg, b_r, b_i, lam, ts):
    b, s, d = x3.shape
    assert b == SUBLANES and d == LRU_BLOCKS * LANES and w.shape[1] == 5 * d + 2 * KV_DIM
    nt = s // ts
    nh = s // SUBLANES
    r = ts // SUBLANES
    tile = lambda c: pl.BlockSpec((b, ts, c), lambda i: (0, i, 0))
    _, act_width = _act_cols(d)
    return pl.pallas_call(
        functools.partial(_front_kernel, ts),
        grid=(nt,),
        in_specs=[tile(d),
                  pl.BlockSpec((b, SUBLANES, d), lambda i: (0, jnp.minimum((i + 1) * r, nh - 1), 0)),
                  _const_spec((1, d)), _const_spec(w.shape),
                  _const_spec((CONV_WIDTH, d)), _const_spec((1, d)), _const_spec(wg.shape),
                  _const_spec((1, d)), _const_spec((1, d)), _const_spec((1, d))],
        out_specs=tile(act_width),
        out_shape=jax.ShapeDtypeStruct((b, s, act_width), BF16),
        scratch_shapes=_lru_scratch(ts, b),
        compiler_params=_cparams(("arbitrary",)),
        name="front",
    )(x3, x3, g.reshape(1, d), w, conv_w, conv_b.reshape(1, d), wg, b_r.reshape(1, d), b_i.reshape(1, d),
      lam.reshape(1, d))


def _attn_kernel(tq, sink_ref, q_ref, kvp_ref, kvc_ref, kvn_ref, bias_ref, o_ref, kv_scr):
    i = pl.program_id(1)
    nt = pl.num_programs(1)
    nblk = tq // BLOCK
    kv_scr[0:BLOCK] = kvp_ref[...]
    kv_scr[BLOCK:BLOCK + tq] = kvc_ref[...]
    kv_scr[BLOCK + tq:] = kvn_ref[...]
    for j in range(nblk):
        for hh in range(N_HEADS):
            g = hh // GROUP
            c0 = (hh % GROUP) * BLOCK
            q = q_ref[j * BLOCK:(j + 1) * BLOCK, hh * HEAD_DIM:(hh + 1) * HEAD_DIM]
            k = kv_scr[j * BLOCK:(j + 3) * BLOCK, g * HEAD_DIM:(g + 1) * HEAD_DIM]
            v = kv_scr[j * BLOCK:(j + 3) * BLOCK, KV_DIM + g * HEAD_DIM:KV_DIM + (g + 1) * HEAD_DIM]
            st = lax.dot_general(k, q, (((1,), (1,)), ((), ())), preferred_element_type=F32)
            logits = st + bias_ref[g, :, c0:c0 + BLOCK]
            if j == 0:
                top = jnp.where(i == 0, -jnp.inf, logits[:BLOCK])
                logits = jnp.concatenate([top, logits[BLOCK:]], axis=0)
            if j == nblk - 1:
                bot = jnp.where(i == nt - 1, -jnp.inf, logits[2 * BLOCK:])
                logits = jnp.concatenate([logits[:2 * BLOCK], bot], axis=0)
            sink = jnp.full((1, BLOCK), sink_ref[hh] * LOG2E, F32)
            m = jnp.maximum(jnp.max(logits, axis=0, keepdims=True), sink)
            p = jnp.exp2(logits - m)
            denom = jnp.sum(p, axis=0, keepdims=True) + jnp.exp2(sink - m)
            pn = (p * (1.0 / denom)).astype(BF16)
            o = lax.dot_general(pn, v, (((0,), (0,)), ((), ())), preferred_element_type=F32)
            o_ref[j * BLOCK:(j + 1) * BLOCK, hh * HEAD_DIM:(hh + 1) * HEAD_DIM] = o.astype(BF16)


def _attention(act, d, sink, bias_tab, b, s, tq):
    t = act.shape[0]
    col, _ = _act_cols(d)
    qc = col["q"] // d
    kvc = col["kv"] // (2 * KV_DIM)
    nt = s // tq
    nb = s // BLOCK
    r = tq // BLOCK
    return pl.pallas_call(
        functools.partial(_attn_kernel, tq),
        grid=(b, nt),
        in_specs=[pl.BlockSpec(memory_space=pltpu.SMEM),
                  pl.BlockSpec((tq, d), lambda bi, i: (bi * nt + i, qc)),
                  pl.BlockSpec((BLOCK, 2 * KV_DIM), lambda bi, i: (bi * nb + jnp.maximum(i * r - 1, 0), kvc)),
                  pl.BlockSpec((tq, 2 * KV_DIM), lambda bi, i: (bi * nt + i, kvc)),
                  pl.BlockSpec((BLOCK, 2 * KV_DIM),
                               lambda bi, i: (bi * nb + jnp.minimum((i + 1) * r, nb - 1), kvc)),
                  _const_spec(bias_tab.shape)],
        out_specs=pl.BlockSpec((tq, d), lambda bi, i: (bi * nt + i, 0)),
        out_shape=jax.ShapeDtypeStruct((t, d), BF16),
        scratch_shapes=[pltpu.VMEM((tq + 2 * BLOCK, 2 * KV_DIM), BF16)],
        compiler_params=_cparams(("parallel", "parallel")),
        name="attention",
    )(sink.astype(F32), act, act, act, act, bias_tab)


def _back_kernel(ts, x_ref, at_ref, gab_ref, xp_ref, lru_ref, xn_ref,
                 cw_ref, cb_ref, wg_ref, br_ref, bi_ref, lam_ref, woa_ref, wol_ref, wout_ref,
                 o_ref, x_tm, h_tm, carry_scr, y_scr, y_new):
    i = pl.program_id(0)
    nt = pl.num_programs(0) - 1
    t_idx = jnp.maximum(nt - 1 - i, 0)
    nb, _, d = x_ref.shape
    rows = ts * nb
    lo = CONV_LEFT * nb

    @pl.when(i == 0)
    def _():
        carry_scr[...] = jnp.zeros_like(carry_scr)
        y_scr[...] = jnp.zeros_like(y_scr)

    for b in range(nb):
        prev = jnp.where(t_idx > 0, xp_ref[b, BF16_ROWS - SUBLANES:, :].astype(F32), 0.0)
        nxt = jnp.where(t_idx < nt - 1, xn_ref[b, :SUBLANES, :].astype(F32), 0.0)
        for c in range(LRU_BLOCKS):
            cs = slice(c * LANES, (c + 1) * LANES)
            x_tm[c, pl.ds(lo + b, ts, stride=nb), :] = lru_ref[b, :, d + c * LANES:d + (c + 1) * LANES].astype(F32)
            x_tm[c, pl.ds(b, CONV_LEFT, stride=nb), :] = prev[SUBLANES - CONV_LEFT:, cs]
            x_tm[c, pl.ds(lo + rows + b, CONV_RIGHT, stride=nb), :] = nxt[:CONV_RIGHT, cs]

    attn = jnp.concatenate([at_ref[b] for b in range(nb)], axis=0)
    half = LRU_BLOCKS // 2
    cw_ = d // half
    ya_parts, yb_parts = [], []

    def after_block(c):
        cs = slice(c * LANES, (c + 1) * LANES)
        for b in range(nb):
            hb = h_tm[c, pl.ds(b, ts, stride=nb), :]
            hf = lru_ref[b, :, 2 * d + c * LANES:2 * d + (c + 1) * LANES].astype(F32)
            yv = (hf + hb) * _gelu_tanh(lru_ref[b, :, cs].astype(F32))
            y_new[b * ts:(b + 1) * ts, cs] = yv.astype(BF16)
        k = c % half
        cols = slice(k * cw_, (k + 1) * cw_)
        if c < half:
            ya_parts.append(jnp.dot(attn, woa_ref[:, cols], preferred_element_type=F32))
        else:
            yb_parts.append(jnp.dot(y_scr[...], wol_ref[:, cols], preferred_element_type=F32))

    _lru_core(True, ts, nb, x_tm, h_tm, carry_scr,
              cw_ref, cb_ref, wg_ref, br_ref, bi_ref, lam_ref, after_block)

    ya = jnp.concatenate(ya_parts, axis=1)
    yb = jnp.concatenate(yb_parts, axis=1)
    ga = jnp.concatenate([gab_ref[b, :, :d] for b in range(nb)], axis=0).astype(F32)
    gb = jnp.concatenate([gab_ref[b, :, d:] for b in range(nb)], axis=0).astype(F32)
    mix = jax.nn.sigmoid(ga) * ya + jax.nn.sigmoid(gb) * yb
    upd = jnp.dot(mix.astype(BF16), wout_ref[...], preferred_element_type=F32)
    for b in range(nb):
        o_ref[b] = x_ref[b] + upd[b * ts:(b + 1) * ts]

    y_scr[...] = y_new[...]


def _back(x3, attn3, act3, conv_w, conv_b, wg, b_r, b_i, lam, woa, wol, wout, ts):
    b, s, d = x3.shape
    col, _ = _act_cols(d)
    assert col["gr"] == 0 and col["xr"] == d and col["hf"] == 2 * d and col["gb"] == col["ga"] + d
    nt = s // ts
    nh = s // BF16_ROWS
    r = ts // BF16_ROWS
    tix = lambda i: jnp.maximum(nt - 1 - i, 0)
    mix_tix = lambda i: jnp.minimum(nt - i, nt - 1)
    mix_tile = pl.BlockSpec((b, ts, d), lambda i: (0, mix_tix(i), 0))
    xr_halo = lambda tb: pl.BlockSpec((b, BF16_ROWS, d), lambda i: (0, tb(i), col["xr"] // d))
    return pl.pallas_call(
        functools.partial(_back_kernel, ts),
        grid=(nt + 1,),
        in_specs=[mix_tile, mix_tile,
                  pl.BlockSpec((b, ts, 2 * d), lambda i: (0, mix_tix(i), col["ga"] // (2 * d))),
                  xr_halo(lambda i: jnp.maximum(tix(i) * r - 1, 0)),
                  pl.BlockSpec((b, ts, 3 * d), lambda i: (0, tix(i), 0)),
                  xr_halo(lambda i: jnp.minimum((tix(i) + 1) * r, nh - 1)),
                  _const_spec((CONV_WIDTH, d)), _const_spec((1, d)), _const_spec(wg.shape),
                  _const_spec((1, d)), _const_spec((1, d)), _const_spec((1, d)),
                  _const_spec(woa.shape), _const_spec(wol.shape), _const_spec(wout.shape)],
        out_specs=mix_tile,
        out_shape=jax.ShapeDtypeStruct((b, s, d), F32),
        scratch_shapes=_lru_scratch(ts, b) + [pltpu.VMEM((ts * b, d), BF16)] * 2,
        compiler_params=_cparams(("arbitrary",)),
        name="back",
    )(x3, attn3, act3, act3, act3, act3, conv_w, conv_b.reshape(1, d), wg,
      b_r.reshape(1, d), b_i.reshape(1, d), lam.reshape(1, d), woa, wol, wout)


def _mlp_kernel(ff_chunk, final, x_ref, g2_ref, wup_ref, wdn_ref, gf_ref, o_ref):
    x1 = x_ref[...]
    h2 = _rms(x1, g2_ref[...]).astype(BF16)
    acc = x1
    dff = wup_ref.shape[1]
    for c0 in range(0, dff, ff_chunk):
        u = jnp.dot(h2, wup_ref[:, c0:c0 + ff_chunk], preferred_element_type=F32)
        u = jnp.square(jnp.maximum(u, 0.0)).astype(BF16)
        acc = acc + jnp.dot(u, wdn_ref[c0:c0 + ff_chunk, :], preferred_element_type=F32)
    if final:
        acc = _rms(acc, gf_ref[...])
    o_ref[...] = acc


def _mlp(xf, g2, wup, wdn, gf, final, tm):
    t, d = xf.shape
    tile = pl.BlockSpec((tm, d), lambda i: (i, 0))
    return pl.pallas_call(
        functools.partial(_mlp_kernel, min(1024, wup.shape[1]), final),
        grid=(t // tm,),
        in_specs=[tile, _const_spec((1, d)), _const_spec(wup.shape), _const_spec(wdn.shape),
                  _const_spec((1, d))],
        out_specs=tile,
        out_shape=jax.ShapeDtypeStruct((t, d), F32),
        compiler_params=_cparams(("parallel",)),
        name="mlp",
    )(xf, g2.reshape(1, d), wup, wdn, gf.reshape(1, d))


def _tile(n, pref):
    t = min(pref, n)
    assert n % t == 0, (n, t)
    return t


def kernel(x, w_in, w_o_attn, w_o_lru, w_out, attn_sink, rel_bias, conv_w, conv_b, w_rgate, b_rgate,
           w_igate, b_igate, lru_lambda, norm1_g, norm2_g, w_mlp_up, w_mlp_down, final_norm_g):
    b, s, d = x.shape
    depth = w_in.shape[0]
    assert d == N_HEADS * HEAD_DIM and s % BLOCK == 0
    t = b * s
    tm = _tile(t, 1024)
    tq = _tile(s, 1024)
    ts = _tile(s, 64)

    x3 = x
    bias_tab = _bias_table(rel_bias)
    for l in range(depth):
        wg = [jnp.concatenate([w_rgate[l, dr], w_igate[l, dr]], axis=-1).astype(BF16) for dr in range(2)]
        act = _front(x3, norm1_g[l], w_in[l].astype(BF16), conv_w[l], conv_b[l], wg[0],
                     b_rgate[l, 0], b_igate[l, 0], lru_lambda[l, 0], ts)
        attn = _attention(act.reshape(t, act.shape[-1]), d, attn_sink[l], bias_tab, b, s, tq)
        x1 = _back(x3, attn.reshape(b, s, d), act, conv_w[l], conv_b[l], wg[1],
                   b_rgate[l, 1], b_igate[l, 1], lru_lambda[l, 1], w_o_attn[l].astype(BF16),
                   w_o_lru[l].astype(BF16), w_out[l].astype(BF16), ts)
        x3 = _mlp(x1.reshape(t, d), norm2_g[l], w_mlp_up[l].astype(BF16), w_mlp_down[l].astype(BF16),
                  final_norm_g, l == depth - 1, tm).reshape(b, s, d)
    return x3
```
